```python
import math
import jax, jax.numpy as jnp
from jax import lax
import numpy as np

D_MODEL = 1024
BATCH = 8
SEQ = 2048
DEPTH = 4
DEC_BATCH = 128
DEC_SEQ = 8
PAST_LEN = 2048
PAGE_SIZE = 128

HEAD_DIM = 64
ATT_WIDTH = D_MODEL // 2
N_HEADS = ATT_WIDTH // HEAD_DIM
SSM_GROUP_CH = 16
SSM_WIDTH = D_MODEL // 2
SSM_GROUPS = SSM_WIDTH // SSM_GROUP_CH
SSM_STATE = 64
D_FF = -(-8 * D_MODEL // (3 * 256)) * 256
IN_COLS = 3 * ATT_WIDTH + SSM_WIDTH + 2 * D_MODEL
Q_BLOCK = 128
RMS_EPS = 1e-6
DT_MIN = 1e-3
DT_MAX = 1e-1
SB_BIAS_INIT = -6.0

kernel_name = "s5_stickbreaking_gated_hybrid_step"


def _rmsnorm(x, g):
    xf = x.astype(jnp.float32)
    y = xf * lax.rsqrt(jnp.mean(xf * xf, axis=-1, keepdims=True) + RMS_EPS)
    return (y * g.astype(jnp.float32)).astype(x.dtype)


def _ssm_discretize(a_re, a_im, log_dt, b_re, b_im):
    a_re = a_re.astype(jnp.float32)
    a_im = a_im.astype(jnp.float32)
    dt = jnp.exp(log_dt.astype(jnp.float32))[:, None]
    mag = jnp.exp(a_re * dt)
    ang = a_im * dt
    ab_re = mag * jnp.cos(ang)
    ab_im = mag * jnp.sin(ang)
    nr = ab_re - 1.0
    den = a_re * a_re + a_im * a_im
    c_re = (nr * a_re + ab_im * a_im) / den
    c_im = (ab_im * a_re - nr * a_im) / den
    b_re = b_re.astype(jnp.float32)
    b_im = b_im.astype(jnp.float32)
    bb_re = c_re[..., None] * b_re - c_im[..., None] * b_im
    bb_im = c_re[..., None] * b_im + c_im[..., None] * b_re
    return ab_re, ab_im, bb_re, bb_im


def _linear_recurrence_combine(e1, e2):
    a1r, a1i, b1r, b1i = e1
    a2r, a2i, b2r, b2i = e2
    return (a1r * a2r - a1i * a2i,
            a1r * a2i + a1i * a2r,
            a2r * b1r - a2i * b1i + b2r,
            a2r * b1i + a2i * b1r + b2i)


def _s5(u, h0_re, h0_im, a_re, a_im, log_dt, b_re, b_im, c_re, c_im, d_skip, w_glu):
    bsz, t = u.shape[0], u.shape[1]
    uf = u.astype(jnp.float32)
    ug = uf.reshape(bsz, t, SSM_GROUPS, SSM_GROUP_CH)
    ab_re, ab_im, bb_re, bb_im = _ssm_discretize(a_re, a_im, log_dt, b_re, b_im)
    bu_re = jnp.einsum('btgc,gnc->btgn', ug, bb_re)
    bu_im = jnp.einsum('btgc,gnc->btgn', ug, bb_im)
    h0r = h0_re.astype(jnp.float32)
    h0i = h0_im.astype(jnp.float32)
    bu_re = bu_re.at[:, 0].add(ab_re * h0r - ab_im * h0i)
    bu_im = bu_im.at[:, 0].add(ab_re * h0i + ab_im * h0r)
    a_re_t = jnp.broadcast_to(ab_re, bu_re.shape)
    a_im_t = jnp.broadcast_to(ab_im, bu_im.shape)
    _, _, h_re, h_im = lax.associative_scan(
        _linear_recurrence_combine, (a_re_t, a_im_t, bu_re, bu_im), axis=1)
    y = (jnp.einsum('gcn,btgn->btgc', c_re.astype(jnp.float32), h_re)
         - jnp.einsum('gcn,btgn->btgc', c_im.astype(jnp.float32), h_im))
    y = y.reshape(bsz, t, SSM_WIDTH) + d_skip.astype(jnp.float32) * uf
    yg = jax.nn.gelu(y)
    out = yg * jax.nn.sigmoid(yg @ w_glu.astype(jnp.float32))
    return (out.astype(u.dtype),
            h_re[:, -1].astype(h0_re.dtype),
            h_im[:, -1].astype(h0_im.dtype))


def _sb_block(q, qpos, k, v, kpos, sb_bias):
    z = (jnp.einsum('bqhd,bkhd->bhqk', q, k).astype(jnp.float32) * (HEAD_DIM ** -0.5)
         + sb_bias.astype(jnp.float32)[None, :, None, None])
    mask = kpos[None, :] < qpos[:, None]
    log_keep = jnp.where(mask, jax.nn.log_sigmoid(-z), 0.0)
    log_after = lax.cumsum(log_keep, axis=3, reverse=True) - log_keep
    w = jnp.where(mask, jnp.exp(jax.nn.log_sigmoid(z) + log_after), 0.0)
    return jnp.einsum('bhqk,bkhd->bqhd', w.astype(v.dtype), v)


def _stick_breaking(q, k, v, q_offset, sb_bias):
    bsz, tq = q.shape[0], q.shape[1]
    qpos = q_offset + jnp.arange(tq, dtype=jnp.int32)
    kpos = jnp.arange(k.shape[1], dtype=jnp.int32)
    if tq % Q_BLOCK == 0 and tq > Q_BLOCK:
        nb = tq // Q_BLOCK
        qb = q.reshape(bsz, nb, Q_BLOCK, N_HEADS, HEAD_DIM).transpose(1, 0, 2, 3, 4)
        pb = qpos.reshape(nb, Q_BLOCK)
        ob = lax.map(lambda a: _sb_block(a[0], a[1], k, v, kpos, sb_bias), (qb, pb))
        return ob.transpose(1, 0, 2, 3, 4).reshape(bsz, tq, N_HEADS, HEAD_DIM)
    return _sb_block(q, qpos, k, v, kpos, sb_bias)


def _layer(x, h0_re, h0_im, past_k, past_v,
           norm1_g, w_in, sb_bias, a_re, a_im, log_dt, b_re, b_im, c_re, c_im, d_skip, w_glu,
           w_attn_up, w_ssm_up, w_out, norm2_g, w_ffn_gate, w_ffn_up, w_ffn_down):
    bsz, t = x.shape[0], x.shape[1]
    h = _rmsnorm(x, norm1_g)
    proj = h @ w_in
    q, k, v, u, g_att, g_ssm = jnp.split(
        proj, [ATT_WIDTH, 2 * ATT_WIDTH, 3 * ATT_WIDTH, 3 * ATT_WIDTH + SSM_WIDTH,
               3 * ATT_WIDTH + SSM_WIDTH + D_MODEL], axis=-1)
    q = q.reshape(bsz, t, N_HEADS, HEAD_DIM)
    k = k.reshape(bsz, t, N_HEADS, HEAD_DIM)
    v = v.reshape(bsz, t, N_HEADS, HEAD_DIM)
    if past_k is None:
        k_all, v_all, offset = k, v, 0
    else:
        k_all = jnp.concatenate([past_k.astype(k.dtype), k], axis=1)
        v_all = jnp.concatenate([past_v.astype(v.dtype), v], axis=1)
        offset = past_k.shape[1]
    o_att = _stick_breaking(q, k_all, v_all, offset, sb_bias).reshape(bsz, t, ATT_WIDTH)
    y_ssm, h_re, h_im = _s5(u, h0_re, h0_im, a_re, a_im, log_dt, b_re, b_im,
                            c_re, c_im, d_skip, w_glu)
    merged = (jax.nn.sigmoid(g_att) * (o_att @ w_attn_up)
              + jax.nn.sigmoid(g_ssm) * (y_ssm @ w_ssm_up))
    x = x + merged @ w_out
    h2 = _rmsnorm(x, norm2_g)
    x = x + (jax.nn.silu(h2 @ w_ffn_gate) * (h2 @ w_ffn_up)) @ w_ffn_down
    return x, k, v, h_re, h_im


def setup_inputs(seed: int = 0) -> dict:
    key = jax.random.key(seed)
    ks = jax.random.split(key, 32)
    f32 = jnp.float32
    n_pages = PAST_LEN // PAGE_SIZE
    n_phys = (5 * DEC_BATCH * n_pages) // 4

    def nrm(k, shape, scale):
        return jax.random.normal(k, shape, f32) * scale

    page_table = jax.random.permutation(ks[0], n_phys)[:DEC_BATCH * n_pages]
    page_table = page_table.reshape(DEC_BATCH, n_pages).astype(jnp.int32)
    n_idx = jnp.arange(SSM_STATE, dtype=f32)
    return {
        "x_prompt": nrm(ks[1], (BATCH, SEQ, D_MODEL), 1.0),
        "x_sample": nrm(ks[2], (DEC_BATCH, DEC_SEQ, D_MODEL), 1.0),
        "cache_k": nrm(ks[3], (DEPTH, n_phys, PAGE_SIZE, N_HEADS, HEAD_DIM), 1.0),
        "cache_v": nrm(ks[4], (DEPTH, n_phys, PAGE_SIZE, N_HEADS, HEAD_DIM), 1.0),
        "state_ssm_re": nrm(ks[5], (DEPTH, DEC_BATCH, SSM_GROUPS, SSM_STATE), 0.1),
        "state_ssm_im": nrm(ks[6], (DEPTH, DEC_BATCH, SSM_GROUPS, SSM_STATE), 0.1),
        "page_table": page_table,
        "norm1_g": 1.0 + nrm(ks[7], (DEPTH, D_MODEL), 0.02),
        "w_in": nrm(ks[8], (DEPTH, D_MODEL, IN_COLS), D_MODEL ** -0.5),
        "sb_bias": SB_BIAS_INIT + nrm(ks[26], (DEPTH, N_HEADS), 0.1),
        "ssm_a_re": -0.5 + nrm(ks[9], (DEPTH, SSM_GROUPS, SSM_STATE), 0.01),
        "ssm_a_im": math.pi * n_idx + nrm(ks[10], (DEPTH, SSM_GROUPS, SSM_STATE), 0.01),
        "ssm_log_dt": jax.random.uniform(ks[11], (DEPTH, SSM_GROUPS), f32,
                                         minval=math.log(DT_MIN), maxval=math.log(DT_MAX)),
        "ssm_b_re": nrm(ks[12], (DEPTH, SSM_GROUPS, SSM_STATE, SSM_GROUP_CH), (2 * SSM_GROUP_CH) ** -0.5),
        "ssm_b_im": nrm(ks[13], (DEPTH, SSM_GROUPS, SSM_STATE, SSM_GROUP_CH), (2 * SSM_GROUP_CH) ** -0.5),
        "ssm_c_re": nrm(ks[14], (DEPTH, SSM_GROUPS, SSM_GROUP_CH, SSM_STATE), SSM_STATE ** -0.5),
        "ssm_c_im": nrm(ks[15], (DEPTH, SSM_GROUPS, SSM_GROUP_CH, SSM_STATE), SSM_STATE ** -0.5),
        "ssm_d": nrm(ks[16], (DEPTH, SSM_WIDTH), 1.0),
        "ssm_w_glu": nrm(ks[17], (DEPTH, SSM_WIDTH, SSM_WIDTH), SSM_WIDTH ** -0.5),
        "w_attn_up": nrm(ks[18], (DEPTH, ATT_WIDTH, D_MODEL), ATT_WIDTH ** -0.5),
        "w_ssm_up": nrm(ks[19], (DEPTH, SSM_WIDTH, D_MODEL), SSM_WIDTH ** -0.5),
        "w_out": nrm(ks[20], (DEPTH, D_MODEL, D_MODEL), D_MODEL ** -0.5),
        "norm2_g": 1.0 + nrm(ks[21], (DEPTH, D_MODEL), 0.02),
        "w_ffn_gate": nrm(ks[22], (DEPTH, D_MODEL, D_FF), D_MODEL ** -0.5),
        "w_ffn_up": nrm(ks[23], (DEPTH, D_MODEL, D_FF), D_MODEL ** -0.5),
        "w_ffn_down": nrm(ks[24], (DEPTH, D_FF, D_MODEL), D_FF ** -0.5),
        "final_norm_g": 1.0 + nrm(ks[25], (D_MODEL,), 0.02),
    }


def reference(x_prompt, x_sample, cache_k, cache_v, state_ssm_re, state_ssm_im, page_table,
              norm1_g, w_in, sb_bias, ssm_a_re, ssm_a_im, ssm_log_dt, ssm_b_re, ssm_b_im,
              ssm_c_re, ssm_c_im, ssm_d, ssm_w_glu, w_attn_up, w_ssm_up, w_out,
              norm2_g, w_ffn_gate, w_ffn_up, w_ffn_down, final_norm_g):
    n_dec, n_pages = page_table.shape
    past_len = n_pages * cache_k.shape[2]
    xp, xs = x_prompt, x_sample
    h_zero = jnp.zeros((xp.shape[0], SSM_GROUPS, SSM_STATE), state_ssm_re.dtype)
    kp_l, vp_l, ks_l, vs_l = [], [], [], []
    hrp_l, hip_l, hrs_l, his_l = [], [], [], []
    for l in range(DEPTH):
        lp = (norm1_g[l], w_in[l], sb_bias[l], ssm_a_re[l], ssm_a_im[l], ssm_log_dt[l],
              ssm_b_re[l], ssm_b_im[l], ssm_c_re[l], ssm_c_im[l], ssm_d[l], ssm_w_glu[l],
              w_attn_up[l], w_ssm_up[l], w_out[l], norm2_g[l], w_ffn_gate[l], w_ffn_up[l],
              w_ffn_down[l])
        xp, kp, vp, hrp, hip = _layer(xp, h_zero, h_zero, None, None, *lp)
        past_k = cache_k[l][page_table].reshape(n_dec, past_len, N_HEADS, HEAD_DIM)
        past_v = cache_v[l][page_table].reshape(n_dec, past_len, N_HEADS, HEAD_DIM)
        xs, ksm, vsm, hrs, his = _layer(xs, state_ssm_re[l], state_ssm_im[l],
                                        past_k, past_v, *lp)
        kp_l.append(kp); vp_l.append(vp); ks_l.append(ksm); vs_l.append(vsm)
        hrp_l.append(hrp); hip_l.append(hip); hrs_l.append(hrs); his_l.append(his)
    y_prompt = _rmsnorm(xp, final_norm_g)
    y_sample = _rmsnorm(xs, final_norm_g)
    return (y_prompt, y_sample,
            jnp.stack(kp_l), jnp.stack(vp_l), jnp.stack(ks_l), jnp.stack(vs_l),
            jnp.stack(hrp_l), jnp.stack(hip_l), jnp.stack(hrs_l), jnp.stack(his_l))
```

```python
import functools
import math

import jax
import jax.numpy as jnp
from jax import lax
from jax.experimental import pallas as pl
from jax.experimental.pallas import tpu as pltpu

F32 = jnp.float32
BF16 = jnp.bfloat16

HEAD_DIM = 64
SSM_GROUP_CH = 16
RMS_EPS = 1e-6
ROW_TILE = 512
ATT_BLOCK = 256
SSM_SEQ_TILE = 8
SSM_TIME_TILE = 64
SSM_COL_TILE = 512
VMEM_LIMIT = 56 * 1024 * 1024


def _cparams(sem):
    return pltpu.CompilerParams(dimension_semantics=sem, vmem_limit_bytes=VMEM_LIMIT)


def _const_spec(shape):
    nd = len(shape)
    return pl.BlockSpec(shape, lambda *_: (0,) * nd)


def _rmsnorm(x, g):
    ms = jnp.mean(x * x, axis=-1, keepdims=True)
    return x * lax.rsqrt(ms + RMS_EPS) * g


def _bdot(a, b):
    return jnp.dot(a.astype(BF16), b.astype(BF16), preferred_element_type=F32)


def _log_sigmoid(z):
    return jnp.minimum(z, 0.0) - jnp.log1p(jnp.exp(-jnp.abs(z)))


def _split_dot(x, m):
    hi = x.astype(BF16)
    lo = (x - hi.astype(F32)).astype(BF16)
    return (jnp.dot(hi, m, preferred_element_type=F32)
            + jnp.dot(lo, m, preferred_element_type=F32))


def _inproj_kernel(x_ref, g_ref, w_ref, q_ref, k_ref, v_ref, kb_ref, vb_ref, u_ref,
                   ga_ref, gs_ref, *, aw, sw, d):
    h = _rmsnorm(x_ref[...], g_ref[...]).astype(BF16)

    def proj(c0, n):
        return jnp.dot(h, w_ref[:, c0:c0 + n], preferred_element_type=F32)

    q_ref[...] = proj(0, aw) * (HEAD_DIM ** -0.5)
    k = proj(aw, aw)
    k_ref[...] = k
    kb_ref[...] = k.astype(BF16)
    v = proj(2 * aw, aw)
    v_ref[...] = v
    vb_ref[...] = v.astype(BF16)
    u_ref[...] = proj(3 * aw, sw)
    ga_ref[...] = proj(3 * aw + sw, d)
    gs_ref[...] = proj(3 * aw + sw + d, d)


def _inproj(x, g, w_in, aw, sw):
    n, d = x.shape
    cols = w_in.shape[1]
    row = lambda w: pl.BlockSpec((ROW_TILE, w), lambda i: (i, 0))
    out_shape = (
        jax.ShapeDtypeStruct((n, aw), F32),
        jax.ShapeDtypeStruct((n, aw), F32),
        jax.ShapeDtypeStruct((n, aw), F32),
        jax.ShapeDtypeStruct((n, aw), BF16),
        jax.ShapeDtypeStruct((n, aw), BF16),
        jax.ShapeDtypeStruct((n, sw), F32),
        jax.ShapeDtypeStruct((n, d), F32),
        jax.ShapeDtypeStruct((n, d), F32),
    )
    return pl.pallas_call(
        functools.partial(_inproj_kernel, aw=aw, sw=sw, d=d),
        grid=(n // ROW_TILE,),
        in_specs=[row(d), _const_spec((1, d)), _const_spec((d, cols))],
        out_specs=(row(aw), row(aw), row(aw), row(aw), row(aw), row(sw), row(d), row(d)),
        out_shape=out_shape,
        compiler_params=_cparams(("parallel",)),
        name="inproj",
    )(x, g, w_in)


def _sb_weights(z, run, tri, mask):
    ls = _log_sigmoid(z)
    lk = ls - z
    if mask is not None:
        lk = jnp.where(mask, lk, 0.0)
    la = _split_dot(lk, tri) + run
    w = jnp.exp(ls + la)
    if mask is not None:
        w = jnp.where(mask, w, 0.0)
    return w, run + jnp.sum(lk, axis=-1, keepdims=True)


def _later_key_matrix(n):
    r = lax.broadcasted_iota(jnp.int32, (n, n), 0)
    c = lax.broadcasted_iota(jnp.int32, (n, n), 1)
    return (r > c).astype(BF16)


def _attn_prompt_kernel(bias_ref, q_ref, k_ref, v_ref, o_ref):
    hp = pl.program_id(1)
    qi = pl.program_id(2)
    blk = ATT_BLOCK
    q = q_ref[...].astype(BF16)
    lane = lax.broadcasted_iota(jnp.int32, q.shape, 1)
    tri = _later_key_matrix(blk)
    r = lax.broadcasted_iota(jnp.int32, (blk, blk), 0)
    c = lax.broadcasted_iota(jnp.int32, (blk, blk), 1)
    causal = c < r
    nt = (((1,), (1,)), ((), ()))

    def head(hh):
        in_head = (lane < HEAD_DIM) if hh == 0 else (lane >= HEAD_DIM)
        qh = jnp.where(in_head, q, jnp.zeros_like(q))
        bias = bias_ref[2 * hp + hh]

        def block(j, acc, run, mask):
            start = pl.multiple_of(j * blk, blk)
            kb = k_ref[pl.ds(start, blk), :]
            vb = v_ref[pl.ds(start, blk), :]
            z = lax.dot_general(qh, kb, nt, preferred_element_type=F32) + bias
            w, run = _sb_weights(z, run, tri, mask)
            acc = acc + jnp.dot(w.astype(BF16), vb, preferred_element_type=F32)
            return acc, run

        acc, run = block(qi, jnp.zeros((blk, 2 * HEAD_DIM), F32), jnp.zeros((blk, 1), F32), causal)

        def body(i, carry):
            return block(qi - 1 - i, carry[0], carry[1], None)

        acc, _ = lax.fori_loop(0, qi, body, (acc, run))
        return acc, in_head

    acc0, first = head(0)
    acc1, _ = head(1)
    o_ref[...] = jnp.where(first, acc0, acc1).astype(o_ref.dtype)


def _attn_prompt(bias, q, kb, vb, n_seq, seq):
    n, aw = q.shape
    blk = ATT_BLOCK
    nq = seq // blk
    pair = 2 * HEAD_DIM
    return pl.pallas_call(
        _attn_prompt_kernel,
        grid=(n_seq, aw // pair, nq),
        in_specs=[
            pl.BlockSpec(memory_space=pltpu.SMEM),
            pl.BlockSpec((blk, pair), lambda b, h, i: (b * nq + i, h)),
            pl.BlockSpec((seq, pair), lambda b, h, i: (b, h)),
            pl.BlockSpec((seq, pair), lambda b, h, i: (b, h)),
        ],
        out_specs=pl.BlockSpec((blk, pair), lambda b, h, i: (b * nq + i, h)),
        out_shape=jax.ShapeDtypeStruct((n_seq * seq, aw), BF16),
        compiler_params=_cparams(("parallel", "parallel", "arbitrary")),
        name="attn_prompt",
    )(bias, q, kb, vb)


def _attn_sample_kernel(pt_ref, bias_ref, q_ref, kn_ref, vn_ref, *refs, n_pages, n_heads, page):
    del pt_ref
    k_refs = refs[:n_pages]
    v_refs = refs[n_pages:2 * n_pages]
    o_ref = refs[2 * n_pages]
    tq = q_ref.shape[0]
    hd = HEAD_DIM

    def heads(x):
        return jnp.stack([x[:, h * hd:(h + 1) * hd] for h in range(n_heads)])

    q3 = heads(q_ref[...]).astype(BF16)
    pad = jnp.zeros((n_heads, page - tq, hd), F32)
    kn3 = jnp.concatenate([heads(kn_ref[...]), pad], axis=1)
    vn3 = jnp.concatenate([heads(vn_ref[...]), pad], axis=1)

    hidx = lax.broadcasted_iota(jnp.int32, (n_heads, 1, 1), 0)
    bias = jnp.zeros((n_heads, 1, 1), F32)
    for h in range(n_heads):
        bias = jnp.where(hidx == h, bias_ref[h], bias)

    tri = _later_key_matrix(page)
    rows = n_heads * tq
    batch = ((0,), (0,))
    nn = (((2,), (1,)), batch)
    nt = (((2,), (2,)), batch)

    def block(k3, v3, k_dims, v_dims, acc, run, mask):
        z = lax.dot_general(q3, k3.astype(BF16), k_dims, preferred_element_type=F32) + bias
        w, run = _sb_weights(z.reshape(rows, page), run,
                             tri, None if mask is None else mask.reshape(rows, page))
        w3 = w.reshape(n_heads, tq, page).astype(BF16)
        acc = acc + lax.dot_general(w3, v3.astype(BF16), v_dims, preferred_element_type=F32)
        return acc, run

    qpos = lax.broadcasted_iota(jnp.int32, (n_heads, tq, page), 1)
    kpos = lax.broadcasted_iota(jnp.int32, (n_heads, tq, page), 2)
    acc = jnp.zeros((n_heads, tq, hd), F32)
    run = jnp.zeros((rows, 1), F32)
    acc, run = block(kn3, vn3, nt, nn, acc, run, kpos < qpos)
    for j in reversed(range(n_pages)):
        acc, run = block(k_refs[j][...], v_refs[j][...], nn, nt, acc, run, None)
    o_ref[...] = jnp.concatenate([acc[h] for h in range(n_heads)], axis=-1).astype(o_ref.dtype)


def _attn_sample(page_table, bias, q, k, v, cache_k, cache_v, layer, row0, tq):
    n_dec, n_pages = page_table.shape
    _, _, n_heads, hd, page = cache_k.shape
    aw = n_heads * hd
    rb0 = row0 // tq

    def page_spec(j):
        return pl.BlockSpec((None, None, n_heads, hd, page),
                            lambda b, pt: (layer, pt[b * n_pages + j], 0, 0, 0))

    row = pl.BlockSpec((tq, aw), lambda b, pt: (rb0 + b, 0))
    grid_spec = pltpu.PrefetchScalarGridSpec(
        num_scalar_prefetch=1,
        grid=(n_dec,),
        in_specs=[pl.BlockSpec(memory_space=pltpu.SMEM), row, row, row]
        + [page_spec(j) for j in range(n_pages)] * 2,
        out_specs=pl.BlockSpec((tq, aw), lambda b, pt: (b, 0)),
    )
    return pl.pallas_call(
        functools.partial(_attn_sample_kernel, n_pages=n_pages, n_heads=n_heads, page=page),
        grid_spec=grid_spec,
        out_shape=jax.ShapeDtypeStruct((n_dec * tq, aw), BF16),
        compiler_params=_cparams(("parallel",)),
        name="attn_sample",
    )(page_table.reshape(-1), bias, q, k, v, *([cache_k] * n_pages), *([cache_v] * n_pages))


def _discretize_kernel(are_ref, aim_ref, ldt_ref, bre_ref, bim_ref,
                       abre_ref, abim_ref, bbre_ref, bbim_ref):
    a_re = are_ref[...]
    a_im = aim_ref[...]
    dt = jnp.exp(ldt_ref[...])
    mag = jnp.exp(a_re * dt)
    ang = a_im * dt
    ab_re = mag * jnp.cos(ang)
    ab_im = mag * jnp.sin(ang)
    nr = ab_re - 1.0
    den = a_re * a_re + a_im * a_im
    c_re = (nr * a_re + ab_im * a_im) / den
    c_im = (ab_im * a_re - nr * a_im) / den
    b_re = bre_ref[...]
    b_im = bim_ref[...]
    abre_ref[...] = ab_re
    abim_ref[...] = ab_im
    bbre_ref[...] = c_re * b_re - c_im * b_im
    bbim_ref[...] = c_re * b_im + c_im * b_re


def _discretize(a_re, a_im, log_dt, b_re, b_im):
    depth, g, _, n = a_re.shape
    c = b_re.shape[2]
    sa = pl.BlockSpec((None, g, 1, n), lambda l: (l, 0, 0, 0))
    sb = pl.BlockSpec((None, g, c, n), lambda l: (l, 0, 0, 0))
    return pl.pallas_call(
        _discretize_kernel,
        grid=(depth,),
        in_specs=[sa, sa, pl.BlockSpec((None, g, 1, 1), lambda l: (l, 0, 0, 0)), sb, sb],
        out_specs=(sa, sa, sb, sb),
        out_shape=(jax.ShapeDtypeStruct((depth, g, 1, n), F32),) * 2
        + (jax.ShapeDtypeStruct((depth, g, c, n), F32),) * 2,
        compiler_params=_cparams(("parallel",)),
        name="ssm_discretize",
    )(a_re, a_im, log_dt, b_re, b_im)


def _ssm_kernel(u_ref, h0re_ref, h0im_ref, are_ref, aim_ref, bbd_ref, cre_ref, cim_ref,
                d_ref, wglu_ref, y_ref, hre_ref, him_ref, bu_ref, cre_s, cim_s, *, n_tiles, steps):
    tc = pl.program_id(1)
    ns = are_ref.shape[1]
    st = SSM_SEQ_TILE

    @pl.when(tc == 0)
    def _():
        cre_s[...] = h0re_ref[...]
        cim_s[...] = h0im_ref[...]

    u = u_ref[...]
    bu_ref[...] = _bdot(u, bbd_ref[...])

    for cg in range(ns // SSM_COL_TILE):
        re = slice(cg * SSM_COL_TILE, (cg + 1) * SSM_COL_TILE)
        im = slice(ns + cg * SSM_COL_TILE, ns + (cg + 1) * SSM_COL_TILE)
        ar = jnp.broadcast_to(are_ref[:, re], (st, SSM_COL_TILE))
        ai = jnp.broadcast_to(aim_ref[:, re], (st, SSM_COL_TILE))

        def tile(bt, _):
            s0 = pl.multiple_of(bt * st, st)

            def step(t, carry):
                hr, hi = carry
                r0 = pl.multiple_of((t * n_tiles + bt) * st, st)
                nhr = ar * hr - ai * hi + bu_ref[pl.ds(r0, st), re]
                nhi = ar * hi + ai * hr + bu_ref[pl.ds(r0, st), im]
                bu_ref[pl.ds(r0, st), re] = nhr
                bu_ref[pl.ds(r0, st), im] = nhi
                return nhr, nhi

            hr, hi = lax.fori_loop(0, steps, step,
                                   (cre_s[pl.ds(s0, st), re], cim_s[pl.ds(s0, st), re]),
                                   unroll=8)
            cre_s[pl.ds(s0, st), re] = hr
            cim_s[pl.ds(s0, st), re] = hi
            return 0

        lax.fori_loop(0, n_tiles, tile, 0)

    y = (_bdot(bu_ref[:, :ns], cre_ref[...]) - _bdot(bu_ref[:, ns:], cim_ref[...])
         + d_ref[...] * u)
    yg = jax.nn.gelu(y)
    y_ref[...] = (yg * jax.nn.sigmoid(_bdot(yg, wglu_ref[...]))).astype(y_ref.dtype)

    @pl.when(tc == pl.num_programs(1) - 1)
    def _():
        hre_ref[...] = cre_s[...]
        him_ref[...] = cim_s[...]


def _ssm(u_tb, h0_re, h0_im, ab_re, ab_im, bbd, cre_bd, cim_bd, d_skip, w_glu, n_tiles, steps):
    n, sw = u_tb.shape
    n_seq, ns = h0_re.shape
    seqs = n_tiles * SSM_SEQ_TILE
    rows = steps * seqs
    seq_blocks = n_seq // seqs
    time_blocks = n // (rows * seq_blocks)
    assert seq_blocks == 1 or time_blocks == 1
    row_map = (lambda s, t: (t, 0)) if seq_blocks == 1 else (lambda s, t: (s, 0))
    state = pl.BlockSpec((seqs, ns), lambda s, t: (s, 0))
    return pl.pallas_call(
        functools.partial(_ssm_kernel, n_tiles=n_tiles, steps=steps),
        grid=(seq_blocks, time_blocks),
        in_specs=[
            pl.BlockSpec((rows, sw), row_map), state, state,
            _const_spec((1, ns)), _const_spec((1, ns)),
            _const_spec((sw, 2 * ns)), _const_spec((ns, sw)), _const_spec((ns, sw)),
            _const_spec((1, sw)), _const_spec((sw, sw)),
        ],
        out_specs=(pl.BlockSpec((rows, sw), row_map), state, state),
        out_shape=(jax.ShapeDtypeStruct((n, sw), BF16),
                   jax.ShapeDtypeStruct((n_seq, ns), F32),
                   jax.ShapeDtypeStruct((n_seq, ns), F32)),
        scratch_shapes=[pltpu.VMEM((rows, 2 * ns), F32),
                        pltpu.VMEM((seqs, ns), F32), pltpu.VMEM((seqs, ns), F32)],
        compiler_params=_cparams(("parallel", "arbitrary")),
        name="ssm",
    )(u_tb, h0_re, h0_im, ab_re, ab_im, bbd, cre_bd, cim_bd, d_skip, w_glu)


def _merge_kernel(o_ref, y_ref, ga_ref, gs_ref, x_ref, wa_ref, ws_ref, wo_ref, out_ref):
    att = jnp.dot(o_ref[...], wa_ref[...], preferred_element_type=F32)
    ssm = jnp.dot(y_ref[...], ws_ref[...], preferred_element_type=F32)
    merged = jax.nn.sigmoid(ga_ref[...]) * att + jax.nn.sigmoid(gs_ref[...]) * ssm
    out_ref[...] = x_ref[...] + _bdot(merged, wo_ref[...])


def _merge(o_att, y_ssm, g_att, g_ssm, x, w_attn_up, w_ssm_up, w_out):
    n, d = x.shape
    aw = o_att.shape[1]
    sw = y_ssm.shape[1]
    row = lambda w: pl.BlockSpec((ROW_TILE, w), lambda i: (i, 0))
    return pl.pallas_call(
        _merge_kernel,
        grid=(n // ROW_TILE,),
        in_specs=[row(aw), row(sw), row(d), row(d), row(d),
                  _const_spec((aw, d)), _const_spec((sw, d)), _const_spec((d, d))],
        out_specs=row(d),
        out_shape=jax.ShapeDtypeStruct((n, d), F32),
        compiler_params=_cparams(("parallel",)),
        name="merge",
    )(o_att, y_ssm, g_att, g_ssm, x, w_attn_up, w_ssm_up, w_out)


def _ffn_chunks(d_ff):
    chunk = 768
    edges = list(range(0, d_ff, chunk)) + [d_ff]
    return list(zip(edges[:-1], edges[1:]))


def _ffn_kernel(x_ref, g_ref, wg_ref, wu_ref, wd_ref, *rest, final):
    out_ref = rest[-1]
    x = x_ref[...]
    h = _rmsnorm(x, g_ref[...]).astype(BF16)
    acc = x
    for c0, c1 in _ffn_chunks(wg_ref.shape[1]):
        gate = jnp.dot(h, wg_ref[:, c0:c1], preferred_element_type=F32)
        up = jnp.dot(h, wu_ref[:, c0:c1], preferred_element_type=F32)
        acc = acc + _bdot(jax.nn.silu(gate) * up, wd_ref[c0:c1, :])
    if final:
        acc = _rmsnorm(acc, rest[0][...])
    out_ref[...] = acc


def _ffn(x, g, w_gate, w_up, w_down, final_g=None):
    n, d = x.shape
    d_ff = w_gate.shape[1]
    row = pl.BlockSpec((ROW_TILE, d), lambda i: (i, 0))
    final = final_g is not None
    in_specs = [row, _const_spec((1, d)), _const_spec((d, d_ff)), _const_spec((d, d_ff)),
                _const_spec((d_ff, d))]
    args = [x, g, w_gate, w_up, w_down]
    if final:
        in_specs.append(_const_spec((1, d)))
        args.append(final_g)
    return pl.pallas_call(
        functools.partial(_ffn_kernel, final=final),
        grid=(n // ROW_TILE,),
        in_specs=in_specs,
        out_specs=row,
        out_shape=jax.ShapeDtypeStruct((n, d), F32),
        compiler_params=_cparams(("parallel",)),
        name="ffn",
    )(*args)


def _block_diag(m):
    g, r, c = m.shape
    eye = jnp.eye(g, dtype=m.dtype)
    return (eye[:, None, :, None] * m[:, :, None, :]).reshape(g * r, g * c)


def _time_major(a, n_seq, seq):
    return a.reshape(n_seq, seq, -1).transpose(1, 0, 2).reshape(n_seq * seq, -1)


def _seq_major(a, n_seq, seq):
    return a.reshape(seq, n_seq, -1).transpose(1, 0, 2).reshape(n_seq * seq, -1)


def kernel(x_prompt, x_sample, cache_k, cache_v, state_ssm_re, state_ssm_im, page_table, norm1_g, w_in, sb_bias, ssm_a_re, ssm_a_im, ssm_log_dt, ssm_b_re, ssm_b_im, ssm_c_re, ssm_c_im, ssm_d, ssm_w_glu, w_attn_up, w_ssm_up, w_out, norm2_g, w_ffn_gate, w_ffn_up, w_ffn_down, final_norm_g):
    n_seq_p, seq_p, d = x_prompt.shape
    n_seq_s, seq_s, _ = x_sample.shape
    depth = w_in.shape[0]
    _, groups, n_state = ssm_a_re.shape
    aw = w_attn_up.shape[1]
    sw = w_ssm_up.shape[1]
    n_heads = aw // HEAD_DIM
    ns = groups * n_state
    n_p = n_seq_p * seq_p
    n_s = n_seq_s * seq_s

    x = jnp.concatenate([x_prompt.reshape(n_p, d), x_sample.reshape(n_s, d)], axis=0)

    ab_re, ab_im, bb_re, bb_im = _discretize(
        ssm_a_re[:, :, None, :], ssm_a_im[:, :, None, :], ssm_log_dt[:, :, None, None],
        ssm_b_re.transpose(0, 1, 3, 2), ssm_b_im.transpose(0, 1, 3, 2))

    cache_kt = cache_k.transpose(0, 1, 3, 4, 2)
    cache_vt = cache_v.transpose(0, 1, 3, 4, 2)

    zeros_p = jnp.zeros((n_seq_p, ns), F32)
    outs = [[] for _ in range(8)]
    for l in range(depth):
        bbd = jnp.concatenate([_block_diag(bb_re[l]), _block_diag(bb_im[l])], axis=1).astype(BF16)
        cre_bd = _block_diag(ssm_c_re[l].transpose(0, 2, 1)).astype(BF16)
        cim_bd = _block_diag(ssm_c_im[l].transpose(0, 2, 1)).astype(BF16)
        a_re_l = ab_re[l].reshape(1, ns)
        a_im_l = ab_im[l].reshape(1, ns)
        d_l = ssm_d[l].reshape(1, sw)
        wglu_l = ssm_w_glu[l].astype(BF16)

        q, k, v, kb, vb, u, g_att, g_ssm = _inproj(
            x, norm1_g[l].reshape(1, d), w_in[l].astype(BF16), aw, sw)

        o_p = _attn_prompt(sb_bias[l], q, kb, vb, n_seq_p, seq_p)
        o_s = _attn_sample(page_table, sb_bias[l], q, k, v, cache_kt, cache_vt, l, n_p, seq_s)
        o_att = jnp.concatenate([o_p, o_s], axis=0)

        ssm_w = (a_re_l, a_im_l, bbd, cre_bd, cim_bd, d_l, wglu_l)
        y_p, hre_p, him_p = _ssm(_time_major(u[:n_p], n_seq_p, seq_p), zeros_p, zeros_p, *ssm_w,
                                 n_tiles=n_seq_p // SSM_SEQ_TILE, steps=SSM_TIME_TILE)
        y_s, hre_s, him_s = _ssm(_time_major(u[n_p:], n_seq_s, seq_s),
                                 state_ssm_re[l].reshape(n_seq_s, ns),
                                 state_ssm_im[l].reshape(n_seq_s, ns), *ssm_w,
                                 n_tiles=n_seq_s // SSM_SEQ_TILE, steps=seq_s)
        y_ssm = jnp.concatenate([_seq_major(y_p, n_seq_p, seq_p),
                                 _seq_major(y_s, n_seq_s, seq_s)], axis=0)

        x = _merge(o_att, y_ssm, g_att, g_ssm, x, w_attn_up[l].astype(BF16),
                   w_ssm_up[l].astype(BF16), w_out[l].astype(BF16))
        x = _ffn(x, norm2_g[l].reshape(1, d), w_ffn_gate[l].astype(BF16),
                 w_ffn_up[l].astype(BF16), w_ffn_down[l].astype(BF16),
                 final_norm_g.reshape(1, d) if l == depth - 1 else None)

        for dst, val in zip(outs, (k[:n_p], v[:n_p], k[n_p:], v[n_p:], hre_p, him_p, hre_s, him_s)):
            dst.append(val)

    kp, vp, ks, vs, hrp, hip, hrs, his = (jnp.stack(o) for o in outs)
    return (x[:n_p].reshape(n_seq_p, seq_p, d),
            x[n_p:].reshape(n_seq_s, seq_s, d),
            kp.reshape(depth, n_seq_p, seq_p, n_heads, HEAD_DIM),
            vp.reshape(depth, n_seq_p, seq_p, n_heads, HEAD_DIM),
            ks.reshape(depth, n_seq_s, seq_s, n_heads, HEAD_DIM),
            vs.reshape(depth, n_seq_s, seq_s, n_heads, HEAD_DIM),
            hrp.reshape(depth, n_seq_p, groups, n_state),
            hip.reshape(depth, n_seq_p, groups, n_state),
            hrs.reshape(depth, n_seq_s, groups, n_state),
            his.reshape(depth, n_seq_s, groups, n_state))
```

```python
import functools
import math

import jax
import jax.numpy as jnp
from jax import lax
from jax.experimental import pallas as pl
from jax.experimental.pallas import tpu as pltpu

F32 = jnp.float32
BF16 = jnp.bfloat16

HEAD_DIM = 64
SSM_GROUP_CH = 16
RMS_EPS = 1e-6
LOG2_E = 1.0 / math.log(2.0)
ROW_TILE = 512
ATT_BLOCK = 256
ATT_HEADS = 8
SSM_SEQ_TILE = 8
SSM_TIME_TILE = 64
SSM_COL_TILE = 512
VMEM_LIMIT = 56 * 1024 * 1024


def _cparams(sem):
    return pltpu.CompilerParams(dimension_semantics=sem, vmem_limit_bytes=VMEM_LIMIT)


def _const_spec(shape):
    nd = len(shape)
    return pl.BlockSpec(shape, lambda *_: (0,) * nd)


def _rmsnorm(x, g):
    ms = jnp.mean(x * x, axis=-1, keepdims=True)
    return x * lax.rsqrt(ms + RMS_EPS) * g


def _bdot(a, b):
    return jnp.dot(a.astype(BF16), b.astype(BF16), preferred_element_type=F32)


def _sb_log2_terms(z):
    t = z * LOG2_E
    log_beta = jnp.minimum(t, 0.0) - jnp.log2(1.0 + jnp.exp2(-jnp.abs(t)))
    return log_beta, t - log_beta


def _split_dot(x, m):
    hi = x.astype(BF16)
    lo = (x - hi.astype(F32)).astype(BF16)
    return (jnp.dot(hi, m, preferred_element_type=F32)
            + jnp.dot(lo, m, preferred_element_type=F32))


def _inproj_kernel(x_ref, g_ref, w_ref, q_ref, k_ref, v_ref, kb_ref, vb_ref, u_ref,
                   ga_ref, gs_ref, *, aw, sw, d):
    h = _rmsnorm(x_ref[...], g_ref[...]).astype(BF16)

    def proj(c0, n):
        return jnp.dot(h, w_ref[:, c0:c0 + n], preferred_element_type=F32)

    q_ref[...] = proj(0, aw) * (HEAD_DIM ** -0.5)
    k = proj(aw, aw)
    k_ref[...] = k
    kb_ref[...] = k.astype(BF16)
    v = proj(2 * aw, aw)
    v_ref[...] = v
    vb_ref[...] = v.astype(BF16)
    u_ref[...] = proj(3 * aw, sw)
    ga_ref[...] = proj(3 * aw + sw, d)
    gs_ref[...] = proj(3 * aw + sw + d, d)


def _inproj(x, g, w_in, aw, sw):
    n, d = x.shape
    cols = w_in.shape[1]
    row = lambda w: pl.BlockSpec((ROW_TILE, w), lambda i: (i, 0))
    out_shape = (
        jax.ShapeDtypeStruct((n, aw), F32),
        jax.ShapeDtypeStruct((n, aw), F32),
        jax.ShapeDtypeStruct((n, aw), F32),
        jax.ShapeDtypeStruct((n, aw), BF16),
        jax.ShapeDtypeStruct((n, aw), BF16),
        jax.ShapeDtypeStruct((n, sw), F32),
        jax.ShapeDtypeStruct((n, d), F32),
        jax.ShapeDtypeStruct((n, d), F32),
    )
    return pl.pallas_call(
        functools.partial(_inproj_kernel, aw=aw, sw=sw, d=d),
        grid=(n // ROW_TILE,),
        in_specs=[row(d), _const_spec((1, d)), _const_spec((d, cols))],
        out_specs=(row(aw), row(aw), row(aw), row(aw), row(aw), row(sw), row(d), row(d)),
        out_shape=out_shape,
        compiler_params=_cparams(("parallel",)),
        name="inproj",
    )(x, g, w_in)


def _later_key_matrix(n):
    r = lax.broadcasted_iota(jnp.int32, (n, n), 0)
    c = lax.broadcasted_iota(jnp.int32, (n, n), 1)
    return (r > c).astype(BF16)


def _attn_prompt_kernel(bias_ref, q_ref, k_ref, v_ref, o_ref, acc_ref, run_ref):
    grp = pl.program_id(1)
    qi = pl.program_id(2)
    blk = ATT_BLOCK
    pair = 2 * HEAD_DIM
    n_pairs = ATT_HEADS // 2
    lane = lax.broadcasted_iota(jnp.int32, (blk, pair), 1)
    in_head = (lane < HEAD_DIM, lane >= HEAD_DIM)
    tri = _later_key_matrix(blk)
    r = lax.broadcasted_iota(jnp.int32, (blk, blk), 0)
    c = lax.broadcasted_iota(jnp.int32, (blk, blk), 1)
    causal = c < r
    nt = (((1,), (1,)), ((), ()))

    q = q_ref[...].astype(BF16)
    qh = [jnp.where(in_head[h % 2], q[:, (h // 2) * pair:(h // 2 + 1) * pair], 0).astype(BF16)
          for h in range(ATT_HEADS)]
    bias = [bias_ref[grp * ATT_HEADS + h] for h in range(ATT_HEADS)]
    acc_ref[...] = jnp.zeros_like(acc_ref)
    run_ref[...] = jnp.zeros_like(run_ref)

    def block(j, mask):
        start = pl.multiple_of(j * blk, blk)
        heads = range(ATT_HEADS)
        kb = [k_ref[pl.ds(start, blk), p * pair:(p + 1) * pair] for p in range(n_pairs)]
        vb = [v_ref[pl.ds(start, blk), p * pair:(p + 1) * pair] for p in range(n_pairs)]
        z = [lax.dot_general(qh[h], kb[h // 2], nt, preferred_element_type=F32) for h in heads]
        terms = [_sb_log2_terms(z[h] + bias[h]) for h in heads]
        log_beta = [t[0] for t in terms]
        keep = [t[1] if mask is None else jnp.where(mask, t[1], 0.0) for t in terms]
        later = [_split_dot(keep[h], tri) for h in heads]
        w = [jnp.exp2(log_beta[h] - (later[h] + run_ref[h])) for h in heads]
        if mask is not None:
            w = [jnp.where(mask, w[h], 0.0) for h in heads]
        for h in heads:
            run_ref[h] = run_ref[h] + jnp.sum(keep[h], axis=-1, keepdims=True)
        for p in range(n_pairs):
            pv = [jnp.dot(w[2 * p + hh].astype(BF16), jnp.where(in_head[hh], vb[p], 0).astype(BF16),
                          preferred_element_type=F32) for hh in range(2)]
            acc_ref[p] = acc_ref[p] + (pv[0] + pv[1])

    block(qi, causal)

    def body(i, carry):
        block(qi - 1 - i, None)
        return carry

    lax.fori_loop(0, qi, body, 0)
    o_ref[...] = jnp.concatenate([acc_ref[p] for p in range(n_pairs)], axis=-1).astype(o_ref.dtype)


def _attn_prompt(bias, q, kb, vb, n_seq, seq):
    n, aw = q.shape
    blk = ATT_BLOCK
    nq = seq // blk
    width = ATT_HEADS * HEAD_DIM
    return pl.pallas_call(
        _attn_prompt_kernel,
        grid=(n_seq, aw // width, nq),
        in_specs=[
            pl.BlockSpec(memory_space=pltpu.SMEM),
            pl.BlockSpec((blk, width), lambda b, g, i: (b * nq + i, g)),
            pl.BlockSpec((seq, width), lambda b, g, i: (b, g)),
            pl.BlockSpec((seq, width), lambda b, g, i: (b, g)),
        ],
        out_specs=pl.BlockSpec((blk, width), lambda b, g, i: (b * nq + i, g)),
        out_shape=jax.ShapeDtypeStruct((n_seq * seq, aw), BF16),
        scratch_shapes=[pltpu.VMEM((ATT_HEADS // 2, blk, 2 * HEAD_DIM), F32),
                        pltpu.VMEM((ATT_HEADS, blk, 1), F32)],
        compiler_params=_cparams(("parallel", "parallel", "arbitrary")),
        name="attn_prompt",
    )(bias, q, kb, vb)


def _attn_sample_kernel(pt_ref, bias_ref, q_ref, kn_ref, vn_ref, *refs, n_pages, n_heads, page):
    del pt_ref
    k_refs = refs[:n_pages]
    v_refs = refs[n_pages:2 * n_pages]
    o_ref = refs[2 * n_pages]
    tq = q_ref.shape[0]
    hd = HEAD_DIM
    aw = n_heads * hd
    rows = n_heads * tq
    nt = (((1,), (1,)), ((), ()))

    row_head = lax.broadcasted_iota(jnp.int32, (rows, aw), 0) // tq
    col_head = lax.broadcasted_iota(jnp.int32, (rows, aw), 1) // hd
    q_bd = jnp.where(row_head == col_head,
                     jnp.concatenate([q_ref[...]] * n_heads, axis=0), 0.0).astype(BF16)
    bias_head = lax.broadcasted_iota(jnp.int32, (rows, 1), 0) // tq
    bias = jnp.zeros((rows, 1), F32)
    for h in range(n_heads):
        bias = jnp.where(bias_head == h, bias_ref[h], bias)

    pad = jnp.zeros((page - tq, aw), F32)
    k_new = jnp.concatenate([kn_ref[...], pad], axis=0).astype(BF16)
    v_new = jnp.concatenate([vn_ref[...], pad], axis=0).astype(BF16)
    tri = _later_key_matrix(page)

    def past(ref):
        return ref[...].reshape(aw, page).astype(BF16)

    n_blocks = n_pages + 1
    z = jnp.concatenate(
        [lax.dot_general(q_bd, k_new, nt, preferred_element_type=F32) + bias]
        + [jnp.dot(q_bd, past(k_refs[j]), preferred_element_type=F32) + bias
           for j in reversed(range(n_pages))], axis=0)
    r = lax.broadcasted_iota(jnp.int32, (n_blocks * rows, page), 0)
    c = lax.broadcasted_iota(jnp.int32, (n_blocks * rows, page), 1)
    mask = (r >= rows) | (c < (r % tq))
    log_beta, neg_log_keep = _sb_log2_terms(z)
    neg_log_keep = jnp.where(mask, neg_log_keep, 0.0)
    later = _split_dot(neg_log_keep, tri)
    tot = jnp.sum(neg_log_keep, axis=-1, keepdims=True)
    run = jnp.zeros((rows, 1), F32)
    runs = []
    for b in range(n_blocks):
        runs.append(run)
        run = run + tot[b * rows:(b + 1) * rows]
    w = jnp.where(mask, jnp.exp2(log_beta - later - jnp.concatenate(runs, axis=0)), 0.0)

    w = w.astype(BF16)
    acc = jnp.dot(w[:rows], v_new, preferred_element_type=F32)
    for b in range(1, n_blocks):
        acc = acc + lax.dot_general(w[b * rows:(b + 1) * rows], past(v_refs[n_pages - b]), nt,
                                    preferred_element_type=F32)
    o_ref[...] = jnp.concatenate(
        [acc[h * tq:(h + 1) * tq, h * hd:(h + 1) * hd] for h in range(n_heads)],
        axis=-1).astype(o_ref.dtype)


def _attn_sample(page_table, bias, q, k, v, cache_k, cache_v, layer, row0, tq):
    n_dec, n_pages = page_table.shape
    _, _, n_heads, hd, page = cache_k.shape
    aw = n_heads * hd
    rb0 = row0 // tq

    def page_spec(j):
        return pl.BlockSpec((None, None, n_heads, hd, page),
                            lambda b, pt: (layer, pt[b * n_pages + j], 0, 0, 0))

    row = pl.BlockSpec((tq, aw), lambda b, pt: (rb0 + b, 0))
    grid_spec = pltpu.PrefetchScalarGridSpec(
        num_scalar_prefetch=1,
        grid=(n_dec,),
        in_specs=[pl.BlockSpec(memory_space=pltpu.SMEM), row, row, row]
        + [page_spec(j) for j in range(n_pages)] * 2,
        out_specs=pl.BlockSpec((tq, aw), lambda b, pt: (b, 0)),
    )
    return pl.pallas_call(
        functools.partial(_attn_sample_kernel, n_pages=n_pages, n_heads=n_heads, page=page),
        grid_spec=grid_spec,
        out_shape=jax.ShapeDtypeStruct((n_dec * tq, aw), BF16),
        compiler_params=_cparams(("parallel",)),
        name="attn_sample",
    )(page_table.reshape(-1), bias, q, k, v, *([cache_k] * n_pages), *([cache_v] * n_pages))


def _discretize_kernel(are_ref, aim_ref, ldt_ref, bre_ref, bim_ref,
                       abre_ref, abim_ref, bbre_ref, bbim_ref):
    a_re = are_ref[...]
    a_im = aim_ref[...]
    dt = jnp.exp(ldt_ref[...])
    mag = jnp.exp(a_re * dt)
    ang = a_im * dt
    ab_re = mag * jnp.cos(ang)
    ab_im = mag * jnp.sin(ang)
    nr = ab_re - 1.0
    den = a_re * a_re + a_im * a_im
    c_re = (nr * a_re + ab_im * a_im) / den
    c_im = (ab_im * a_re - nr * a_im) / den
    b_re = bre_ref[...]
    b_im = bim_ref[...]
    abre_ref[...] = ab_re
    abim_ref[...] = ab_im
    bbre_ref[...] = c_re * b_re - c_im * b_im
    bbim_ref[...] = c_re * b_im + c_im * b_re


def _discretize(a_re, a_im, log_dt, b_re, b_im):
    depth, g, _, n = a_re.shape
    c = b_re.shape[2]
    sa = pl.BlockSpec((None, g, 1, n), lambda l: (l, 0, 0, 0))
    sb = pl.BlockSpec((None, g, c, n), lambda l: (l, 0, 0, 0))
    return pl.pallas_call(
        _discretize_kernel,
        grid=(depth,),
        in_specs=[sa, sa, pl.BlockSpec((None, g, 1, 1), lambda l: (l, 0, 0, 0)), sb, sb],
        out_specs=(sa, sa, sb, sb),
        out_shape=(jax.ShapeDtypeStruct((depth, g, 1, n), F32),) * 2
        + (jax.ShapeDtypeStruct((depth, g, c, n), F32),) * 2,
        compiler_params=_cparams(("parallel",)),
        name="ssm_discretize",
    )(a_re, a_im, log_dt, b_re, b_im)


def _ssm_kernel(u_ref, h0re_ref, h0im_ref, are_ref, aim_ref, bbd_ref, cre_ref, cim_ref,
                d_ref, wglu_ref, y_ref, hre_ref, him_ref, bu_ref, cre_s, cim_s, *, n_tiles, steps):
    tc = pl.program_id(1)
    ns = are_ref.shape[1]
    st = SSM_SEQ_TILE

    @pl.when(tc == 0)
    def _():
        cre_s[...] = h0re_ref[...]
        cim_s[...] = h0im_ref[...]

    u = u_ref[...]
    bu_ref[...] = _bdot(u, bbd_ref[...])

    for cg in range(ns // SSM_COL_TILE):
        re = slice(cg * SSM_COL_TILE, (cg + 1) * SSM_COL_TILE)
        im = slice(ns + cg * SSM_COL_TILE, ns + (cg + 1) * SSM_COL_TILE)
        ar = jnp.broadcast_to(are_ref[:, re], (st, SSM_COL_TILE))
        ai = jnp.broadcast_to(aim_ref[:, re], (st, SSM_COL_TILE))

        def tile(bt, _):
            s0 = pl.multiple_of(bt * st, st)

            def step(t, carry):
                hr, hi = carry
                r0 = pl.multiple_of((t * n_tiles + bt) * st, st)
                nhr = ar * hr - ai * hi + bu_ref[pl.ds(r0, st), re]
                nhi = ar * hi + ai * hr + bu_ref[pl.ds(r0, st), im]
                bu_ref[pl.ds(r0, st), re] = nhr
                bu_ref[pl.ds(r0, st), im] = nhi
                return nhr, nhi

            hr, hi = lax.fori_loop(0, steps, step,
                                   (cre_s[pl.ds(s0, st), re], cim_s[pl.ds(s0, st), re]),
                                   unroll=8)
            cre_s[pl.ds(s0, st), re] = hr
            cim_s[pl.ds(s0, st), re] = hi
            return 0

        lax.fori_loop(0, n_tiles, tile, 0)

    y = (_bdot(bu_ref[:, :ns], cre_ref[...]) - _bdot(bu_ref[:, ns:], cim_ref[...])
         + d_ref[...] * u)
    yg = jax.nn.gelu(y)
    y_ref[...] = (yg * jax.nn.sigmoid(_bdot(yg, wglu_ref[...]))).astype(y_ref.dtype)

    @pl.when(tc == pl.num_programs(1) - 1)
    def _():
        hre_ref[...] = cre_s[...]
        him_ref[...] = cim_s[...]


def _ssm(u_tb, h0_re, h0_im, ab_re, ab_im, bbd, cre_bd, cim_bd, d_skip, w_glu, n_tiles, steps):
    n, sw = u_tb.shape
    n_seq, ns = h0_re.shape
    seqs = n_tiles * SSM_SEQ_TILE
    rows = steps * seqs
    seq_blocks = n_seq // seqs
    time_blocks = n // (rows * seq_blocks)
    assert seq_blocks == 1 or time_blocks == 1
    row_map = (lambda s, t: (t, 0)) if seq_blocks == 1 else (lambda s, t: (s, 0))
    state = pl.BlockSpec((seqs, ns), lambda s, t: (s, 0))
    return pl.pallas_call(
        functools.partial(_ssm_kernel, n_tiles=n_tiles, steps=steps),
        grid=(seq_blocks, time_blocks),
        in_specs=[
            pl.BlockSpec((rows, sw), row_map), state, state,
            _const_spec((1, ns)), _const_spec((1, ns)),
            _const_spec((sw, 2 * ns)), _const_spec((ns, sw)), _const_spec((ns, sw)),
            _const_spec((1, sw)), _const_spec((sw, sw)),
        ],
        out_specs=(pl.BlockSpec((rows, sw), row_map), state, state),
        out_shape=(jax.ShapeDtypeStruct((n, sw), BF16),
                   jax.ShapeDtypeStruct((n_seq, ns), F32),
                   jax.ShapeDtypeStruct((n_seq, ns), F32)),
        scratch_shapes=[pltpu.VMEM((rows, 2 * ns), F32),
                        pltpu.VMEM((seqs, ns), F32), pltpu.VMEM((seqs, ns), F32)],
        compiler_params=_cparams(("parallel", "arbitrary")),
        name="ssm",
    )(u_tb, h0_re, h0_im, ab_re, ab_im, bbd, cre_bd, cim_bd, d_skip, w_glu)


def _merge_kernel(op_ref, os_ref, yp_ref, ys_ref, ga_ref, gs_ref, x_ref, wa_ref, ws_ref, wo_ref,
                  out_ref, *, prompt_tiles):
    is_prompt = pl.program_id(0) < prompt_tiles
    o = jnp.where(is_prompt, op_ref[...], os_ref[...])
    y = jnp.where(is_prompt, yp_ref[...], ys_ref[...])
    att = jnp.dot(o, wa_ref[...], preferred_element_type=F32)
    ssm = jnp.dot(y, ws_ref[...], preferred_element_type=F32)
    merged = jax.nn.sigmoid(ga_ref[...]) * att + jax.nn.sigmoid(gs_ref[...]) * ssm
    out_ref[...] = x_ref[...] + _bdot(merged, wo_ref[...])


def _merge(o_p, o_s, y_p, y_s, g_att, g_ssm, x, w_attn_up, w_ssm_up, w_out):
    n, d = x.shape
    aw = o_p.shape[1]
    sw = y_p.shape[1]
    pt = o_p.shape[0] // ROW_TILE
    row = lambda w: pl.BlockSpec((ROW_TILE, w), lambda i: (i, 0))
    prompt = lambda w: pl.BlockSpec((ROW_TILE, w), lambda i: (jnp.minimum(i, pt - 1), 0))
    sample = lambda w: pl.BlockSpec((ROW_TILE, w), lambda i: (jnp.maximum(i - pt, 0), 0))
    return pl.pallas_call(
        functools.partial(_merge_kernel, prompt_tiles=pt),
        grid=(n // ROW_TILE,),
        in_specs=[prompt(aw), sample(aw), prompt(sw), sample(sw), row(d), row(d), row(d),
                  _const_spec((aw, d)), _const_spec((sw, d)), _const_spec((d, d))],
        out_specs=row(d),
        out_shape=jax.ShapeDtypeStruct((n, d), F32),
        compiler_params=_cparams(("parallel",)),
        name="merge",
    )(o_p, o_s, y_p, y_s, g_att, g_ssm, x, w_attn_up, w_ssm_up, w_out)


def _ffn_chunks(d_ff):
    chunk = 768
    edges = list(range(0, d_ff, chunk)) + [d_ff]
    return list(zip(edges[:-1], edges[1:]))


def _ffn_kernel(x_ref, g_ref, wg_ref, wu_ref, wd_ref, *rest, final):
    out_ref = rest[-1]
    x = x_ref[...]
    h = _rmsnorm(x, g_ref[...]).astype(BF16)
    acc = x
    for c0, c1 in _ffn_chunks(wg_ref.shape[1]):
        gate = jnp.dot(h, wg_ref[:, c0:c1], preferred_element_type=F32)
        up = jnp.dot(h, wu_ref[:, c0:c1], preferred_element_type=F32)
        acc = acc + _bdot(jax.nn.silu(gate) * up, wd_ref[c0:c1, :])
    if final:
        acc = _rmsnorm(acc, rest[0][...])
    out_ref[...] = acc


def _ffn(x, g, w_gate, w_up, w_down, final_g=None):
    n, d = x.shape
    d_ff = w_gate.shape[1]
    row = pl.BlockSpec((ROW_TILE, d), lambda i: (i, 0))
    final = final_g is not None
    in_specs = [row, _const_spec((1, d)), _const_spec((d, d_ff)), _const_spec((d, d_ff)),
                _const_spec((d_ff, d))]
    args = [x, g, w_gate, w_up, w_down]
    if final:
        in_specs.append(_const_spec((1, d)))
        args.append(final_g)
    return pl.pallas_call(
        functools.partial(_ffn_kernel, final=final),
        grid=(n // ROW_TILE,),
        in_specs=in_specs,
        out_specs=row,
        out_shape=jax.ShapeDtypeStruct((n, d), F32),
        compiler_params=_cparams(("parallel",)),
        name="ffn",
    )(*args)


def _block_diag(m):
    g, r, c = m.shape
    eye = jnp.eye(g, dtype=m.dtype)
    return (eye[:, None, :, None] * m[:, :, None, :]).reshape(g * r, g * c)


def _time_major(a, n_seq, seq):
    return a.reshape(n_seq, seq, -1).transpose(1, 0, 2).reshape(n_seq * seq, -1)


def _seq_major(a, n_seq, seq):
    return a.reshape(seq, n_seq, -1).transpose(1, 0, 2).reshape(n_seq * seq, -1)


def kernel(x_prompt, x_sample, cache_k, cache_v, state_ssm_re, state_ssm_im, page_table, norm1_g, w_in, sb_bias, ssm_a_re, ssm_a_im, ssm_log_dt, ssm_b_re, ssm_b_im, ssm_c_re, ssm_c_im, ssm_d, ssm_w_glu, w_attn_up, w_ssm_up, w_out, norm2_g, w_ffn_gate, w_ffn_up, w_ffn_down, final_norm_g):
    n_seq_p, seq_p, d = x_prompt.shape
    n_seq_s, seq_s, _ = x_sample.shape
    depth = w_in.shape[0]
    _, groups, n_state = ssm_a_re.shape
    aw = w_attn_up.shape[1]
    sw = w_ssm_up.shape[1]
    n_heads = aw // HEAD_DIM
    ns = groups * n_state
    n_p = n_seq_p * seq_p
    n_s = n_seq_s * seq_s

    x = jnp.concatenate([x_prompt.reshape(n_p, d), x_sample.reshape(n_s, d)], axis=0)

    ab_re, ab_im, bb_re, bb_im = _discretize(
        ssm_a_re[:, :, None, :], ssm_a_im[:, :, None, :], ssm_log_dt[:, :, None, None],
        ssm_b_re.transpose(0, 1, 3, 2), ssm_b_im.transpose(0, 1, 3, 2))

    cache_kt = cache_k.transpose(0, 1, 3, 4, 2)
    cache_vt = cache_v.transpose(0, 1, 3, 4, 2)

    zeros_p = jnp.zeros((n_seq_p, ns), F32)
    outs = [[] for _ in range(8)]
    for l in range(depth):
        bbd = jnp.concatenate([_block_diag(bb_re[l]), _block_diag(bb_im[l])], axis=1).astype(BF16)
        cre_bd = _block_diag(ssm_c_re[l].transpose(0, 2, 1)).astype(BF16)
        cim_bd = _block_diag(ssm_c_im[l].transpose(0, 2, 1)).astype(BF16)
        a_re_l = ab_re[l].reshape(1, ns)
        a_im_l = ab_im[l].reshape(1, ns)
        d_l = ssm_d[l].reshape(1, sw)
        wglu_l = ssm_w_glu[l].astype(BF16)

        q, k, v, kb, vb, u, g_att, g_ssm = _inproj(
            x, norm1_g[l].reshape(1, d), w_in[l].astype(BF16), aw, sw)

        o_p = _attn_prompt(sb_bias[l], q, kb, vb, n_seq_p, seq_p)
        o_s = _attn_sample(page_table, sb_bias[l], q, k, v, cache_kt, cache_vt, l, n_p, seq_s)

        ssm_w = (a_re_l, a_im_l, bbd, cre_bd, cim_bd, d_l, wglu_l)
        y_p, hre_p, him_p = _ssm(_time_major(u[:n_p], n_seq_p, seq_p), zeros_p, zeros_p, *ssm_w,
                                 n_tiles=n_seq_p // SSM_SEQ_TILE, steps=SSM_TIME_TILE)
        y_s, hre_s, him_s = _ssm(_time_major(u[n_p:], n_seq_s, seq_s),
                                 state_ssm_re[l].reshape(n_seq_s, ns),
                                 state_ssm_im[l].reshape(n_seq_s, ns), *ssm_w,
                                 n_tiles=n_seq_s // SSM_SEQ_TILE, steps=seq_s)
        x = _merge(o_p, o_s, _seq_major(y_p, n_seq_p, seq_p), _seq_major(y_s, n_seq_s, seq_s),
                   g_att, g_ssm, x, w_attn_up[l].astype(BF16),
                   w_ssm_up[l].astype(BF16), w_out[l].astype(BF16))
        x = _ffn(x, norm2_g[l].reshape(1, d), w_ffn_gate[l].astype(BF16),
                 w_ffn_up[l].astype(BF16), w_ffn_down[l].astype(BF16),
                 final_norm_g.reshape(1, d) if l == depth - 1 else None)

        for dst, val in zip(outs, (k[:n_p], v[:n_p], k[n_p:], v[n_p:], hre_p, him_p, hre_s, him_s)):
            dst.append(val)

    kp, vp, ks, vs, hrp, hip, hrs, his = (jnp.stack(o) for o in outs)
    return (x[:n_p].reshape(n_seq_p, seq_p, d),
            x[n_p:].reshape(n_seq_s, seq_s, d),
            kp.reshape(depth, n_seq_p, seq_p, n_heads, HEAD_DIM),
            vp.reshape(depth, n_seq_p, seq_p, n_heads, HEAD_DIM),
            ks.reshape(depth, n_seq_s, seq_s, n_heads, HEAD_DIM),
            vs.reshape(depth, n_seq_s, seq_s, n_heads, HEAD_DIM),
            hrp.reshape(depth, n_seq_p, groups, n_state),
            hip.reshape(depth, n_seq_p, groups, n_state),
            hrs.reshape(depth, n_seq_s, groups, n_state),
            his.reshape(depth, n_seq_s, groups, n_state))
```

```python
import functools
import math

import jax
import jax.numpy as jnp
from jax import lax
from jax.experimental import pallas as pl
from jax.experimental.pallas import tpu as pltpu

F32 = jnp.float32
BF16 = jnp.bfloat16

HEAD_DIM = 64
SSM_GROUP_CH = 16
RMS_EPS = 1e-6
LOG2_E = 1.0 / math.log(2.0)
ROW_TILE = 512
ATT_BLOCK = 256
ATT_HEADS = 8
SSM_SEQ_TILE = 8
SSM_TIME_TILE = 64
SSM_COL_TILE = 512
VMEM_LIMIT = 56 * 1024 * 1024


def _cparams(sem):
    return pltpu.CompilerParams(dimension_semantics=sem, vmem_limit_bytes=VMEM_LIMIT)


def _const_spec(shape):
    nd = len(shape)
    return pl.BlockSpec(shape, lambda *_: (0,) * nd)


def _rmsnorm(x, g):
    ms = jnp.mean(x * x, axis=-1, keepdims=True)
    return x * lax.rsqrt(ms + RMS_EPS) * g


def _bdot(a, b):
    return jnp.dot(a.astype(BF16), b.astype(BF16), preferred_element_type=F32)


def _sb_log2_terms(z):
    t = z * LOG2_E
    log_beta = jnp.minimum(t, 0.0) - jnp.log2(1.0 + jnp.exp2(-jnp.abs(t)))
    return log_beta, t - log_beta


def _split_dot(x, m):
    hi = x.astype(BF16)
    lo = (x - hi.astype(F32)).astype(BF16)
    return (jnp.dot(hi, m, preferred_element_type=F32)
            + jnp.dot(lo, m, preferred_element_type=F32))


def _inproj_kernel(x_ref, g_ref, w_ref, wt_ref, *refs, aw, sw, d, prompt_tiles, aliased):
    (q_ref, ks_ref, vs_ref, u_ref, ga_ref, gs_ref, kt_ref, vt_ref) = refs[2 if aliased else 0:]
    i = pl.program_id(0)
    h = _rmsnorm(x_ref[...], g_ref[...]).astype(BF16)

    def proj(c0, n):
        return jnp.dot(h, w_ref[:, c0:c0 + n], preferred_element_type=F32)

    q_ref[...] = proj(0, aw) * (HEAD_DIM ** -0.5)
    u_ref[...] = proj(3 * aw, sw)
    ga_ref[...] = proj(3 * aw + sw, d)
    gs_ref[...] = proj(3 * aw + sw + d, d)

    @pl.when(i < prompt_tiles)
    def _():
        nt = (((1,), (1,)), ((), ()))
        kt_ref[...] = lax.dot_general(wt_ref[:aw, :], h, nt, preferred_element_type=F32)
        vt_ref[...] = lax.dot_general(wt_ref[aw:, :], h, nt, preferred_element_type=F32)

    @pl.when(i >= prompt_tiles)
    def _():
        ks_ref[...] = proj(aw, aw)
        vs_ref[...] = proj(2 * aw, aw)


def _inproj(x, g, w_in, w_kv_t, kv_t, layer, depth, n_seq_p, seq_p, aw, sw):
    n, d = x.shape
    cols = w_in.shape[1]
    seq_tiles = seq_p // ROW_TILE
    pt = n_seq_p * seq_tiles
    n_s = n - pt * ROW_TILE
    aliased = kv_t is not None
    row = lambda w: pl.BlockSpec((ROW_TILE, w), lambda i: (i, 0))
    sample = pl.BlockSpec((ROW_TILE, aw), lambda i: (jnp.maximum(i - pt, 0), 0))

    def kt_map(i):
        j = jnp.minimum(i, pt - 1)
        return (layer, j // seq_tiles, 0, j % seq_tiles)

    kt_spec = pl.BlockSpec((None, None, aw, ROW_TILE), kt_map)
    stacked = jax.ShapeDtypeStruct((depth, n_seq_p, aw, seq_p), F32)
    out_shape = (
        jax.ShapeDtypeStruct((n, aw), F32),
        jax.ShapeDtypeStruct((n_s, aw), F32),
        jax.ShapeDtypeStruct((n_s, aw), F32),
        jax.ShapeDtypeStruct((n, sw), F32),
        jax.ShapeDtypeStruct((n, d), F32),
        jax.ShapeDtypeStruct((n, d), F32),
        stacked, stacked,
    )
    any_spec = pl.BlockSpec(memory_space=pl.ANY)
    return pl.pallas_call(
        functools.partial(_inproj_kernel, aw=aw, sw=sw, d=d, prompt_tiles=pt, aliased=aliased),
        grid=(n // ROW_TILE,),
        in_specs=[row(d), _const_spec((1, d)), _const_spec((d, cols)), _const_spec((2 * aw, d))]
        + ([any_spec, any_spec] if aliased else []),
        out_specs=(row(aw), sample, sample, row(sw), row(d), row(d), kt_spec, kt_spec),
        out_shape=out_shape,
        input_output_aliases={4: 6, 5: 7} if aliased else {},
        compiler_params=_cparams(("arbitrary",)),
        name="inproj",
    )(x, g, w_in, w_kv_t, *(kv_t if aliased else ()))


def _later_key_matrix(n):
    r = lax.broadcasted_iota(jnp.int32, (n, n), 0)
    c = lax.broadcasted_iota(jnp.int32, (n, n), 1)
    return (r > c).astype(BF16)


def _attn_prompt_kernel(bias_ref, q_ref, kt_ref, vt_ref, o_ref, acc_ref, run_ref):
    grp = pl.program_id(1)
    qi = pl.program_id(2)
    blk = ATT_BLOCK
    pair = 2 * HEAD_DIM
    n_pairs = ATT_HEADS // 2
    lane = lax.broadcasted_iota(jnp.int32, (blk, pair), 1)
    in_head = (lane < HEAD_DIM, lane >= HEAD_DIM)
    feat = lax.broadcasted_iota(jnp.int32, (pair, blk), 0)
    v_rows = (feat < HEAD_DIM, feat >= HEAD_DIM)
    tri = _later_key_matrix(blk)
    r = lax.broadcasted_iota(jnp.int32, (blk, blk), 0)
    c = lax.broadcasted_iota(jnp.int32, (blk, blk), 1)
    causal = c < r
    nt = (((1,), (1,)), ((), ()))

    q = q_ref[...].astype(BF16)
    qh = [jnp.where(in_head[h % 2], q[:, (h // 2) * pair:(h // 2 + 1) * pair], 0).astype(BF16)
          for h in range(ATT_HEADS)]
    bias = [bias_ref[grp * ATT_HEADS + h] for h in range(ATT_HEADS)]
    acc_ref[...] = jnp.zeros_like(acc_ref)
    run_ref[...] = jnp.zeros_like(run_ref)

    def block(j, mask):
        start = pl.multiple_of(j * blk, blk)
        heads = range(ATT_HEADS)
        kb = [kt_ref[p * pair:(p + 1) * pair, pl.ds(start, blk)].astype(BF16)
              for p in range(n_pairs)]
        vb = [vt_ref[p * pair:(p + 1) * pair, pl.ds(start, blk)].astype(BF16)
              for p in range(n_pairs)]
        z = [jnp.dot(qh[h], kb[h // 2], preferred_element_type=F32) for h in heads]
        terms = [_sb_log2_terms(z[h] + bias[h]) for h in heads]
        log_beta = [t[0] for t in terms]
        keep = [t[1] if mask is None else jnp.where(mask, t[1], 0.0) for t in terms]
        later = [_split_dot(keep[h], tri) for h in heads]
        w = [jnp.exp2(log_beta[h] - (later[h] + run_ref[h])) for h in heads]
        if mask is not None:
            w = [jnp.where(mask, w[h], 0.0) for h in heads]
        for h in heads:
            run_ref[h] = run_ref[h] + jnp.sum(keep[h], axis=-1, keepdims=True)
        for p in range(n_pairs):
            pv = [lax.dot_general(w[2 * p + hh].astype(BF16),
                                  jnp.where(v_rows[hh], vb[p], 0).astype(BF16), nt,
                                  preferred_element_type=F32) for hh in range(2)]
            acc_ref[p] = acc_ref[p] + (pv[0] + pv[1])

    block(qi, causal)

    def body(i, carry):
        block(qi - 1 - i, None)
        return carry

    lax.fori_loop(0, qi, body, 0)
    o_ref[...] = jnp.concatenate([acc_ref[p] for p in range(n_pairs)], axis=-1).astype(o_ref.dtype)


def _attn_prompt(bias, q, kt, vt, layer):
    _, n_seq, aw, seq = kt.shape
    blk = ATT_BLOCK
    nq = seq // blk
    width = ATT_HEADS * HEAD_DIM
    kv_spec = pl.BlockSpec((None, None, width, seq), lambda b, g, i: (layer, b, g, 0))
    return pl.pallas_call(
        _attn_prompt_kernel,
        grid=(n_seq, aw // width, nq),
        in_specs=[
            pl.BlockSpec(memory_space=pltpu.SMEM),
            pl.BlockSpec((blk, width), lambda b, g, i: (b * nq + i, g)),
            kv_spec, kv_spec,
        ],
        out_specs=pl.BlockSpec((blk, width), lambda b, g, i: (b * nq + i, g)),
        out_shape=jax.ShapeDtypeStruct((n_seq * seq, aw), BF16),
        scratch_shapes=[pltpu.VMEM((ATT_HEADS // 2, blk, 2 * HEAD_DIM), F32),
                        pltpu.VMEM((ATT_HEADS, blk, 1), F32)],
        compiler_params=_cparams(("parallel", "parallel", "arbitrary")),
        name="attn_prompt",
    )(bias, q, kt, vt)


def _attn_sample_kernel(pt_ref, bias_ref, q_ref, kn_ref, vn_ref, *refs, n_pages, n_heads, page):
    del pt_ref
    k_refs = refs[:n_pages]
    v_refs = refs[n_pages:2 * n_pages]
    o_ref = refs[2 * n_pages]
    tq = q_ref.shape[0]
    hd = HEAD_DIM
    aw = n_heads * hd
    rows = n_heads * tq
    nt = (((1,), (1,)), ((), ()))

    row_head = lax.broadcasted_iota(jnp.int32, (rows, aw), 0) // tq
    col_head = lax.broadcasted_iota(jnp.int32, (rows, aw), 1) // hd
    q_bd = jnp.where(row_head == col_head,
                     jnp.concatenate([q_ref[...]] * n_heads, axis=0), 0.0).astype(BF16)
    bias_head = lax.broadcasted_iota(jnp.int32, (rows, 1), 0) // tq
    bias = jnp.zeros((rows, 1), F32)
    for h in range(n_heads):
        bias = jnp.where(bias_head == h, bias_ref[h], bias)

    pad = jnp.zeros((page - tq, aw), F32)
    k_new = jnp.concatenate([kn_ref[...], pad], axis=0).astype(BF16)
    v_new = jnp.concatenate([vn_ref[...], pad], axis=0).astype(BF16)
    tri = _later_key_matrix(page)

    def past(ref):
        return ref[...].reshape(aw, page).astype(BF16)

    n_blocks = n_pages + 1
    z = jnp.concatenate(
        [lax.dot_general(q_bd, k_new, nt, preferred_element_type=F32) + bias]
        + [jnp.dot(q_bd, past(k_refs[j]), preferred_element_type=F32) + bias
           for j in reversed(range(n_pages))], axis=0)
    r = lax.broadcasted_iota(jnp.int32, (n_blocks * rows, page), 0)
    c = lax.broadcasted_iota(jnp.int32, (n_blocks * rows, page), 1)
    mask = (r >= rows) | (c < (r % tq))
    log_beta, neg_log_keep = _sb_log2_terms(z)
    neg_log_keep = jnp.where(mask, neg_log_keep, 0.0)
    later = _split_dot(neg_log_keep, tri)
    tot = jnp.sum(neg_log_keep, axis=-1, keepdims=True)
    run = jnp.zeros((rows, 1), F32)
    runs = []
    for b in range(n_blocks):
        runs.append(run)
        run = run + tot[b * rows:(b + 1) * rows]
    w = jnp.where(mask, jnp.exp2(log_beta - later - jnp.concatenate(runs, axis=0)), 0.0)

    w = w.astype(BF16)
    acc = jnp.dot(w[:rows], v_new, preferred_element_type=F32)
    for b in range(1, n_blocks):
        acc = acc + lax.dot_general(w[b * rows:(b + 1) * rows], past(v_refs[n_pages - b]), nt,
                                    preferred_element_type=F32)
    o_ref[...] = jnp.concatenate(
        [acc[h * tq:(h + 1) * tq, h * hd:(h + 1) * hd] for h in range(n_heads)],
        axis=-1).astype(o_ref.dtype)


def _attn_sample(page_table, bias, q, k, v, cache_k, cache_v, layer, row0, tq):
    n_dec, n_pages = page_table.shape
    _, _, n_heads, hd, page = cache_k.shape
    aw = n_heads * hd
    rb0 = row0 // tq

    def page_spec(j):
        return pl.BlockSpec((None, None, n_heads, hd, page),
                            lambda b, pt: (layer, pt[b * n_pages + j], 0, 0, 0))

    q_row = pl.BlockSpec((tq, aw), lambda b, pt: (rb0 + b, 0))
    kv_row = pl.BlockSpec((tq, aw), lambda b, pt: (b, 0))
    grid_spec = pltpu.PrefetchScalarGridSpec(
        num_scalar_prefetch=1,
        grid=(n_dec,),
        in_specs=[pl.BlockSpec(memory_space=pltpu.SMEM), q_row, kv_row, kv_row]
        + [page_spec(j) for j in range(n_pages)] * 2,
        out_specs=pl.BlockSpec((tq, aw), lambda b, pt: (b, 0)),
    )
    return pl.pallas_call(
        functools.partial(_attn_sample_kernel, n_pages=n_pages, n_heads=n_heads, page=page),
        grid_spec=grid_spec,
        out_shape=jax.ShapeDtypeStruct((n_dec * tq, aw), BF16),
        compiler_params=_cparams(("parallel",)),
        name="attn_sample",
    )(page_table.reshape(-1), bias, q, k, v, *([cache_k] * n_pages), *([cache_v] * n_pages))


def _discretize_kernel(are_ref, aim_ref, ldt_ref, bre_ref, bim_ref,
                       abre_ref, abim_ref, bbre_ref, bbim_ref):
    a_re = are_ref[...]
    a_im = aim_ref[...]
    dt = jnp.exp(ldt_ref[...])
    mag = jnp.exp(a_re * dt)
    ang = a_im * dt
    ab_re = mag * jnp.cos(ang)
    ab_im = mag * jnp.sin(ang)
    nr = ab_re - 1.0
    den = a_re * a_re + a_im * a_im
    c_re = (nr * a_re + ab_im * a_im) / den
    c_im = (ab_im * a_re - nr * a_im) / den
    b_re = bre_ref[...]
    b_im = bim_ref[...]
    abre_ref[...] = ab_re
    abim_ref[...] = ab_im
    bbre_ref[...] = c_re * b_re - c_im * b_im
    bbim_ref[...] = c_re * b_im + c_im * b_re


def _discretize(a_re, a_im, log_dt, b_re, b_im):
    depth, g, _, n = a_re.shape
    c = b_re.shape[2]
    sa = pl.BlockSpec((None, g, 1, n), lambda l: (l, 0, 0, 0))
    sb = pl.BlockSpec((None, g, c, n), lambda l: (l, 0, 0, 0))
    return pl.pallas_call(
        _discretize_kernel,
        grid=(depth,),
        in_specs=[sa, sa, pl.BlockSpec((None, g, 1, 1), lambda l: (l, 0, 0, 0)), sb, sb],
        out_specs=(sa, sa, sb, sb),
        out_shape=(jax.ShapeDtypeStruct((depth, g, 1, n), F32),) * 2
        + (jax.ShapeDtypeStruct((depth, g, c, n), F32),) * 2,
        compiler_params=_cparams(("parallel",)),
        name="ssm_discretize",
    )(a_re, a_im, log_dt, b_re, b_im)


def _ssm_kernel(u_ref, h0re_ref, h0im_ref, are_ref, aim_ref, bbd_ref, cre_ref, cim_ref,
                d_ref, wglu_ref, y_ref, hre_ref, him_ref, bu_ref, cre_s, cim_s, *, n_tiles, steps):
    tc = pl.program_id(1)
    ns = are_ref.shape[1]
    st = SSM_SEQ_TILE

    @pl.when(tc == 0)
    def _():
        cre_s[...] = h0re_ref[...]
        cim_s[...] = h0im_ref[...]

    u = u_ref[...]
    ub = u.astype(BF16)
    n_blocks, cin, _ = bbd_ref.shape
    cs = SSM_COL_TILE
    for cg in range(n_blocks):
        bu_ref[:, 2 * cs * cg:2 * cs * (cg + 1)] = jnp.dot(
            ub[:, cin * cg:cin * (cg + 1)], bbd_ref[cg], preferred_element_type=F32)

    for cg in range(n_blocks):
        nat = slice(cg * cs, (cg + 1) * cs)
        re = slice(2 * cs * cg, 2 * cs * cg + cs)
        im = slice(2 * cs * cg + cs, 2 * cs * (cg + 1))
        ar = jnp.broadcast_to(are_ref[:, nat], (st, cs))
        ai = jnp.broadcast_to(aim_ref[:, nat], (st, cs))

        def tile(bt, _):
            s0 = pl.multiple_of(bt * st, st)

            def step(t, carry):
                hr, hi = carry
                r0 = pl.multiple_of((t * n_tiles + bt) * st, st)
                nhr = ar * hr - ai * hi + bu_ref[pl.ds(r0, st), re]
                nhi = ar * hi + ai * hr + bu_ref[pl.ds(r0, st), im]
                bu_ref[pl.ds(r0, st), re] = nhr
                bu_ref[pl.ds(r0, st), im] = nhi
                return nhr, nhi

            hr, hi = lax.fori_loop(0, steps, step,
                                   (cre_s[pl.ds(s0, st), nat], cim_s[pl.ds(s0, st), nat]),
                                   unroll=8)
            cre_s[pl.ds(s0, st), nat] = hr
            cim_s[pl.ds(s0, st), nat] = hi
            return 0

        lax.fori_loop(0, n_tiles, tile, 0)

    y = jnp.concatenate(
        [_bdot(bu_ref[:, 2 * cs * cg:2 * cs * cg + cs], cre_ref[cg])
         - _bdot(bu_ref[:, 2 * cs * cg + cs:2 * cs * (cg + 1)], cim_ref[cg])
         for cg in range(n_blocks)], axis=-1) + d_ref[...] * u
    yg = jax.nn.gelu(y)
    y_ref[...] = (yg * jax.nn.sigmoid(_bdot(yg, wglu_ref[...]))).astype(y_ref.dtype)

    @pl.when(tc == pl.num_programs(1) - 1)
    def _():
        hre_ref[...] = cre_s[...]
        him_ref[...] = cim_s[...]


def _ssm(u_tb, h0_re, h0_im, ab_re, ab_im, bbd, cre_bd, cim_bd, d_skip, w_glu, n_tiles, steps):
    n, sw = u_tb.shape
    n_seq, ns = h0_re.shape
    seqs = n_tiles * SSM_SEQ_TILE
    rows = steps * seqs
    seq_blocks = n_seq // seqs
    time_blocks = n // (rows * seq_blocks)
    assert seq_blocks == 1 or time_blocks == 1
    row_map = (lambda s, t: (t, 0)) if seq_blocks == 1 else (lambda s, t: (s, 0))
    state = pl.BlockSpec((seqs, ns), lambda s, t: (s, 0))
    return pl.pallas_call(
        functools.partial(_ssm_kernel, n_tiles=n_tiles, steps=steps),
        grid=(seq_blocks, time_blocks),
        in_specs=[
            pl.BlockSpec((rows, sw), row_map), state, state,
            _const_spec((1, ns)), _const_spec((1, ns)),
            _const_spec(bbd.shape), _const_spec(cre_bd.shape), _const_spec(cim_bd.shape),
            _const_spec((1, sw)), _const_spec((sw, sw)),
        ],
        out_specs=(pl.BlockSpec((rows, sw), row_map), state, state),
        out_shape=(jax.ShapeDtypeStruct((n, sw), BF16),
                   jax.ShapeDtypeStruct((n_seq, ns), F32),
                   jax.ShapeDtypeStruct((n_seq, ns), F32)),
        scratch_shapes=[pltpu.VMEM((rows, 2 * ns), F32),
                        pltpu.VMEM((seqs, ns), F32), pltpu.VMEM((seqs, ns), F32)],
        compiler_params=_cparams(("parallel", "arbitrary")),
        name="ssm",
    )(u_tb, h0_re, h0_im, ab_re, ab_im, bbd, cre_bd, cim_bd, d_skip, w_glu)


def _merge_kernel(op_ref, os_ref, yp_ref, ys_ref, ga_ref, gs_ref, x_ref, wa_ref, ws_ref, wo_ref,
                  out_ref, *, prompt_tiles):
    is_prompt = pl.program_id(0) < prompt_tiles
    o = jnp.where(is_prompt, op_ref[...], os_ref[...])
    y = jnp.where(is_prompt, yp_ref[...], ys_ref[...])
    att = jnp.dot(o, wa_ref[...], preferred_element_type=F32)
    ssm = jnp.dot(y, ws_ref[...], preferred_element_type=F32)
    merged = jax.nn.sigmoid(ga_ref[...]) * att + jax.nn.sigmoid(gs_ref[...]) * ssm
    out_ref[...] = x_ref[...] + _bdot(merged, wo_ref[...])


def _merge(o_p, o_s, y_p, y_s, g_att, g_ssm, x, w_attn_up, w_ssm_up, w_out):
    n, d = x.shape
    aw = o_p.shape[1]
    sw = y_p.shape[1]
    pt = o_p.shape[0] // ROW_TILE
    row = lambda w: pl.BlockSpec((ROW_TILE, w), lambda i: (i, 0))
    prompt = lambda w: pl.BlockSpec((ROW_TILE, w), lambda i: (jnp.minimum(i, pt - 1), 0))
    sample = lambda w: pl.BlockSpec((ROW_TILE, w), lambda i: (jnp.maximum(i - pt, 0), 0))
    return pl.pallas_call(
        functools.partial(_merge_kernel, prompt_tiles=pt),
        grid=(n // ROW_TILE,),
        in_specs=[prompt(aw), sample(aw), prompt(sw), sample(sw), row(d), row(d), row(d),
                  _const_spec((aw, d)), _const_spec((sw, d)), _const_spec((d, d))],
        out_specs=row(d),
        out_shape=jax.ShapeDtypeStruct((n, d), F32),
        compiler_params=_cparams(("parallel",)),
        name="merge",
    )(o_p, o_s, y_p, y_s, g_att, g_ssm, x, w_attn_up, w_ssm_up, w_out)


def _ffn_chunks(d_ff):
    chunk = 768
    edges = list(range(0, d_ff, chunk)) + [d_ff]
    return list(zip(edges[:-1], edges[1:]))


def _ffn_kernel(x_ref, g_ref, wg_ref, wu_ref, wd_ref, *rest, final):
    out_ref = rest[-1]
    x = x_ref[...]
    h = _rmsnorm(x, g_ref[...]).astype(BF16)
    acc = x
    for c0, c1 in _ffn_chunks(wg_ref.shape[1]):
        gate = jnp.dot(h, wg_ref[:, c0:c1], preferred_element_type=F32)
        up = jnp.dot(h, wu_ref[:, c0:c1], preferred_element_type=F32)
        acc = acc + _bdot(jax.nn.silu(gate) * up, wd_ref[c0:c1, :])
    if final:
        acc = _rmsnorm(acc, rest[0][...])
    out_ref[...] = acc


def _ffn(x, g, w_gate, w_up, w_down, final_g=None):
    n, d = x.shape
    d_ff = w_gate.shape[1]
    row = pl.BlockSpec((ROW_TILE, d), lambda i: (i, 0))
    final = final_g is not None
    in_specs = [row, _const_spec((1, d)), _const_spec((d, d_ff)), _const_spec((d, d_ff)),
                _const_spec((d_ff, d))]
    args = [x, g, w_gate, w_up, w_down]
    if final:
        in_specs.append(_const_spec((1, d)))
        args.append(final_g)
    return pl.pallas_call(
        functools.partial(_ffn_kernel, final=final),
        grid=(n // ROW_TILE,),
        in_specs=in_specs,
        out_specs=row,
        out_shape=jax.ShapeDtypeStruct((n, d), F32),
        compiler_params=_cparams(("parallel",)),
        name="ffn",
    )(*args)


def _block_diag(m):
    g, r, c = m.shape
    eye = jnp.eye(g, dtype=m.dtype)
    return (eye[:, None, :, None] * m[:, :, None, :]).reshape(g * r, g * c)


def _time_major(a, n_seq, seq):
    return a.reshape(n_seq, seq, -1).transpose(1, 0, 2).reshape(n_seq * seq, -1)


def _seq_major(a, n_seq, seq):
    return a.reshape(seq, n_seq, -1).transpose(1, 0, 2).reshape(n_seq * seq, -1)


def kernel(x_prompt, x_sample, cache_k, cache_v, state_ssm_re, state_ssm_im, page_table, norm1_g, w_in, sb_bias, ssm_a_re, ssm_a_im, ssm_log_dt, ssm_b_re, ssm_b_im, ssm_c_re, ssm_c_im, ssm_d, ssm_w_glu, w_attn_up, w_ssm_up, w_out, norm2_g, w_ffn_gate, w_ffn_up, w_ffn_down, final_norm_g):
    n_seq_p, seq_p, d = x_prompt.shape
    n_seq_s, seq_s, _ = x_sample.shape
    depth = w_in.shape[0]
    _, groups, n_state = ssm_a_re.shape
    aw = w_attn_up.shape[1]
    sw = w_ssm_up.shape[1]
    n_heads = aw // HEAD_DIM
    ns = groups * n_state
    n_p = n_seq_p * seq_p
    n_s = n_seq_s * seq_s

    x = jnp.concatenate([x_prompt.reshape(n_p, d), x_sample.reshape(n_s, d)], axis=0)

    ab_re, ab_im, bb_re, bb_im = _discretize(
        ssm_a_re[:, :, None, :], ssm_a_im[:, :, None, :], ssm_log_dt[:, :, None, None],
        ssm_b_re.transpose(0, 1, 3, 2), ssm_b_im.transpose(0, 1, 3, 2))

    cache_kt = cache_k.transpose(0, 1, 3, 4, 2)
    cache_vt = cache_v.transpose(0, 1, 3, 4, 2)

    gpb = SSM_COL_TILE // n_state
    n_cb = groups // gpb
    blocks = lambda m: jax.vmap(_block_diag)(m.reshape(n_cb, gpb, *m.shape[1:]))

    zeros_p = jnp.zeros((n_seq_p, ns), F32)
    outs = [[] for _ in range(6)]
    kv_t = None
    for l in range(depth):
        bbd = jnp.concatenate([blocks(bb_re[l]), blocks(bb_im[l])], axis=2).astype(BF16)
        cre_bd = blocks(ssm_c_re[l].transpose(0, 2, 1)).astype(BF16)
        cim_bd = blocks(ssm_c_im[l].transpose(0, 2, 1)).astype(BF16)
        a_re_l = ab_re[l].reshape(1, ns)
        a_im_l = ab_im[l].reshape(1, ns)
        d_l = ssm_d[l].reshape(1, sw)
        wglu_l = ssm_w_glu[l].astype(BF16)

        w_in_l = w_in[l].astype(BF16)
        q, k_s, v_s, u, g_att, g_ssm, kt, vt = _inproj(
            x, norm1_g[l].reshape(1, d), w_in_l, w_in_l[:, aw:3 * aw].T, kv_t,
            l, depth, n_seq_p, seq_p, aw, sw)
        kv_t = (kt, vt)

        o_p = _attn_prompt(sb_bias[l], q, kt, vt, l)
        o_s = _attn_sample(page_table, sb_bias[l], q, k_s, v_s, cache_kt, cache_vt, l, n_p, seq_s)

        ssm_w = (a_re_l, a_im_l, bbd, cre_bd, cim_bd, d_l, wglu_l)
        y_p, hre_p, him_p = _ssm(_time_major(u[:n_p], n_seq_p, seq_p), zeros_p, zeros_p, *ssm_w,
                                 n_tiles=n_seq_p // SSM_SEQ_TILE, steps=SSM_TIME_TILE)
        y_s, hre_s, him_s = _ssm(_time_major(u[n_p:], n_seq_s, seq_s),
                                 state_ssm_re[l].reshape(n_seq_s, ns),
                                 state_ssm_im[l].reshape(n_seq_s, ns), *ssm_w,
                                 n_tiles=n_seq_s // SSM_SEQ_TILE, steps=seq_s)
        x = _merge(o_p, o_s, _seq_major(y_p, n_seq_p, seq_p), _seq_major(y_s, n_seq_s, seq_s),
                   g_att, g_ssm, x, w_attn_up[l].astype(BF16),
                   w_ssm_up[l].astype(BF16), w_out[l].astype(BF16))
        x = _ffn(x, norm2_g[l].reshape(1, d), w_ffn_gate[l].astype(BF16),
                 w_ffn_up[l].astype(BF16), w_ffn_down[l].astype(BF16),
                 final_norm_g.reshape(1, d) if l == depth - 1 else None)

        for dst, val in zip(outs, (k_s, v_s, hre_p, him_p, hre_s, him_s)):
            dst.append(val)

    ks, vs, hrp, hip, hrs, his = (jnp.stack(o) for o in outs)
    heads_last = lambda a: a.reshape(depth, n_seq_p, n_heads, HEAD_DIM, seq_p).transpose(0, 1, 4, 2, 3)
    return (x[:n_p].reshape(n_seq_p, seq_p, d),
            x[n_p:].reshape(n_seq_s, seq_s, d),
            heads_last(kv_t[0]),
            heads_last(kv_t[1]),
            ks.reshape(depth, n_seq_s, seq_s, n_heads, HEAD_DIM),
            vs.reshape(depth, n_seq_s, seq_s, n_heads, HEAD_DIM),
            hrp.reshape(depth, n_seq_p, groups, n_state),
            hip.reshape(depth, n_seq_p, groups, n_state),
            hrs.reshape(depth, n_seq_s, groups, n_state),
            his.reshape(depth, n_seq_s, groups, n_state))
```

```python
import functools
import math

import jax
import jax.numpy as jnp
from jax import lax
from jax.experimental import pallas as pl
from jax.experimental.pallas import tpu as pltpu

F32 = jnp.float32
BF16 = jnp.bfloat16

HEAD_DIM = 64
SSM_GROUP_CH = 16
RMS_EPS = 1e-6
LOG2_E = 1.0 / math.log(2.0)
ROW_TILE = 512
ATT_BLOCK = 256
ATT_HEADS = 8
SSM_SEQ_TILE = 8
SSM_TIME_TILE = 64
SSM_COL_TILE = 512
VMEM_LIMIT = 56 * 1024 * 1024


def _cparams(sem):
    return pltpu.CompilerParams(dimension_semantics=sem, vmem_limit_bytes=VMEM_LIMIT)


def _const_spec(shape):
    nd = len(shape)
    return pl.BlockSpec(shape, lambda *_: (0,) * nd)


def _rmsnorm(x, g):
    ms = jnp.mean(x * x, axis=-1, keepdims=True)
    return x * lax.rsqrt(ms + RMS_EPS) * g


def _bdot(a, b):
    return jnp.dot(a.astype(BF16), b.astype(BF16), preferred_element_type=F32)


def _sb_log2_terms(z):
    t = z * LOG2_E
    log_beta = jnp.minimum(t, 0.0) - jnp.log2(1.0 + jnp.exp2(-jnp.abs(t)))
    return log_beta, t - log_beta


def _split_dot(x, m):
    hi = x.astype(BF16)
    lo = (x - hi.astype(F32)).astype(BF16)
    return (jnp.dot(hi, m, preferred_element_type=F32)
            + jnp.dot(lo, m, preferred_element_type=F32))


def _pair_specs(width, prompt_tiles):
    prompt = pl.BlockSpec((ROW_TILE, width), lambda i: (jnp.minimum(i, prompt_tiles - 1), 0))
    sample = pl.BlockSpec((ROW_TILE, width), lambda i: (jnp.maximum(i - prompt_tiles, 0), 0))
    return [prompt, sample]


def _pair_shapes(n_p, n_s, width, dtype):
    return [jax.ShapeDtypeStruct((n_p, width), dtype), jax.ShapeDtypeStruct((n_s, width), dtype)]


def _read_pair(is_prompt, p_ref, s_ref):
    return jnp.where(is_prompt, p_ref[...], s_ref[...])


def _write_pair(is_prompt, p_ref, s_ref, value):
    @pl.when(is_prompt)
    def _():
        p_ref[...] = value

    @pl.when(jnp.logical_not(is_prompt))
    def _():
        s_ref[...] = value


def _inproj_kernel(xp_ref, xs_ref, g_ref, w_ref, wt_ref, *refs, aw, sw, prompt_tiles, aliased):
    (qp_ref, qs_ref, up_ref, us_ref, ks_ref, vs_ref, kt_ref, vt_ref) = refs[2 if aliased else 0:]
    is_prompt = pl.program_id(0) < prompt_tiles
    h = _rmsnorm(_read_pair(is_prompt, xp_ref, xs_ref), g_ref[...]).astype(BF16)

    def proj(c0, n):
        return jnp.dot(h, w_ref[:, c0:c0 + n], preferred_element_type=F32)

    _write_pair(is_prompt, qp_ref, qs_ref, proj(0, aw) * (HEAD_DIM ** -0.5))
    _write_pair(is_prompt, up_ref, us_ref, proj(3 * aw, sw))

    @pl.when(is_prompt)
    def _():
        nt = (((1,), (1,)), ((), ()))
        kt_ref[...] = lax.dot_general(wt_ref[:aw, :], h, nt, preferred_element_type=F32)
        vt_ref[...] = lax.dot_general(wt_ref[aw:, :], h, nt, preferred_element_type=F32)

    @pl.when(jnp.logical_not(is_prompt))
    def _():
        ks_ref[...] = proj(aw, aw)
        vs_ref[...] = proj(2 * aw, aw)


def _inproj(x_p, x_s, g, w_qkvu, w_kv_t, kv_t, layer, depth, n_seq_p, aw, sw):
    n_p, d = x_p.shape
    n_s = x_s.shape[0]
    seq_p = n_p // n_seq_p
    seq_tiles = seq_p // ROW_TILE
    pt = n_p // ROW_TILE
    aliased = kv_t is not None

    def kt_map(i):
        j = jnp.minimum(i, pt - 1)
        return (layer, j // seq_tiles, 0, j % seq_tiles)

    kt_spec = pl.BlockSpec((None, None, aw, ROW_TILE), kt_map)
    stacked = jax.ShapeDtypeStruct((depth, n_seq_p, aw, seq_p), F32)
    sample_kv = _pair_specs(aw, pt)[1]
    any_spec = pl.BlockSpec(memory_space=pl.ANY)
    return pl.pallas_call(
        functools.partial(_inproj_kernel, aw=aw, sw=sw, prompt_tiles=pt, aliased=aliased),
        grid=(pt + n_s // ROW_TILE,),
        in_specs=_pair_specs(d, pt)
        + [_const_spec((1, d)), _const_spec(w_qkvu.shape), _const_spec((2 * aw, d))]
        + ([any_spec, any_spec] if aliased else []),
        out_specs=_pair_specs(aw, pt) + _pair_specs(sw, pt) + [sample_kv, sample_kv, kt_spec, kt_spec],
        out_shape=_pair_shapes(n_p, n_s, aw, F32)
        + _pair_shapes(n_p, n_s, sw, F32)
        + _pair_shapes(n_p, n_s, aw, F32)[1:] * 2
        + [stacked, stacked],
        input_output_aliases={5: 6, 6: 7} if aliased else {},
        compiler_params=_cparams(("arbitrary",)),
        name="inproj",
    )(x_p, x_s, g, w_qkvu, w_kv_t, *(kv_t if aliased else ()))


def _later_key_matrix(n):
    r = lax.broadcasted_iota(jnp.int32, (n, n), 0)
    c = lax.broadcasted_iota(jnp.int32, (n, n), 1)
    return (r > c).astype(BF16)


def _attn_prompt_kernel(bias_ref, q_ref, kt_ref, vt_ref, o_ref, acc_ref, run_ref):
    grp = pl.program_id(1)
    qi = pl.program_id(2)
    blk = ATT_BLOCK
    pair = 2 * HEAD_DIM
    n_pairs = ATT_HEADS // 2
    lane = lax.broadcasted_iota(jnp.int32, (blk, pair), 1)
    in_head = (lane < HEAD_DIM, lane >= HEAD_DIM)
    feat = lax.broadcasted_iota(jnp.int32, (pair, blk), 0)
    v_rows = (feat < HEAD_DIM, feat >= HEAD_DIM)
    tri = _later_key_matrix(blk)
    r = lax.broadcasted_iota(jnp.int32, (blk, blk), 0)
    c = lax.broadcasted_iota(jnp.int32, (blk, blk), 1)
    causal = c < r
    nt = (((1,), (1,)), ((), ()))

    q = q_ref[...].astype(BF16)
    qh = [jnp.where(in_head[h % 2], q[:, (h // 2) * pair:(h // 2 + 1) * pair], 0).astype(BF16)
          for h in range(ATT_HEADS)]
    bias = [bias_ref[grp * ATT_HEADS + h] for h in range(ATT_HEADS)]
    acc_ref[...] = jnp.zeros_like(acc_ref)
    run_ref[...] = jnp.zeros_like(run_ref)

    def block(j, mask):
        start = pl.multiple_of(j * blk, blk)
        heads = range(ATT_HEADS)
        kb = [kt_ref[p * pair:(p + 1) * pair, pl.ds(start, blk)].astype(BF16)
              for p in range(n_pairs)]
        vb = [vt_ref[p * pair:(p + 1) * pair, pl.ds(start, blk)].astype(BF16)
              for p in range(n_pairs)]
        z = [jnp.dot(qh[h], kb[h // 2], preferred_element_type=F32) for h in heads]
        terms = [_sb_log2_terms(z[h] + bias[h]) for h in heads]
        log_beta = [t[0] for t in terms]
        keep = [t[1] if mask is None else jnp.where(mask, t[1], 0.0) for t in terms]
        later = [jnp.dot(keep[h].astype(BF16), tri, preferred_element_type=F32) for h in heads]
        w = [jnp.exp2(log_beta[h] - (later[h] + run_ref[h])) for h in heads]
        if mask is not None:
            w = [jnp.where(mask, w[h], 0.0) for h in heads]
        for h in heads:
            run_ref[h] = run_ref[h] + jnp.sum(keep[h], axis=-1, keepdims=True)
        for p in range(n_pairs):
            pv = [lax.dot_general(w[2 * p + hh].astype(BF16),
                                  jnp.where(v_rows[hh], vb[p], 0).astype(BF16), nt,
                                  preferred_element_type=F32) for hh in range(2)]
            acc_ref[p] = acc_ref[p] + (pv[0] + pv[1])

    block(qi, causal)

    def body(i, carry):
        block(qi - 1 - i, None)
        return carry

    lax.fori_loop(0, qi, body, 0)
    o_ref[...] = jnp.concatenate([acc_ref[p] for p in range(n_pairs)], axis=-1).astype(o_ref.dtype)


def _attn_prompt(bias, q, kt, vt, layer):
    _, n_seq, aw, seq = kt.shape
    blk = ATT_BLOCK
    nq = seq // blk
    width = ATT_HEADS * HEAD_DIM
    kv_spec = pl.BlockSpec((None, None, width, seq), lambda b, g, i: (layer, b, g, 0))
    return pl.pallas_call(
        _attn_prompt_kernel,
        grid=(n_seq, aw // width, nq),
        in_specs=[
            pl.BlockSpec(memory_space=pltpu.SMEM),
            pl.BlockSpec((blk, width), lambda b, g, i: (b * nq + i, g)),
            kv_spec, kv_spec,
        ],
        out_specs=pl.BlockSpec((blk, width), lambda b, g, i: (b * nq + i, g)),
        out_shape=jax.ShapeDtypeStruct((n_seq * seq, aw), BF16),
        scratch_shapes=[pltpu.VMEM((ATT_HEADS // 2, blk, 2 * HEAD_DIM), F32),
                        pltpu.VMEM((ATT_HEADS, blk, 1), F32)],
        compiler_params=_cparams(("parallel", "parallel", "arbitrary")),
        name="attn_prompt",
    )(bias, q, kt, vt)


def _attn_sample_kernel(pt_ref, bias_ref, q_ref, kn_ref, vn_ref, *refs, n_pages, n_heads, page):
    del pt_ref
    k_refs = refs[:n_pages]
    v_refs = refs[n_pages:2 * n_pages]
    o_ref = refs[2 * n_pages]
    tq = q_ref.shape[0]
    hd = HEAD_DIM
    aw = n_heads * hd
    rows = n_heads * tq
    nt = (((1,), (1,)), ((), ()))

    row_head = lax.broadcasted_iota(jnp.int32, (rows, aw), 0) // tq
    col_head = lax.broadcasted_iota(jnp.int32, (rows, aw), 1) // hd
    q_bd = jnp.where(row_head == col_head,
                     jnp.concatenate([q_ref[...]] * n_heads, axis=0), 0.0).astype(BF16)
    bias_head = lax.broadcasted_iota(jnp.int32, (rows, 1), 0) // tq
    bias = jnp.zeros((rows, 1), F32)
    for h in range(n_heads):
        bias = jnp.where(bias_head == h, bias_ref[h], bias)

    pad = jnp.zeros((page - tq, aw), F32)
    k_new = jnp.concatenate([kn_ref[...], pad], axis=0).astype(BF16)
    v_new = jnp.concatenate([vn_ref[...], pad], axis=0).astype(BF16)
    tri = _later_key_matrix(page)

    def past(ref):
        return ref[...].reshape(aw, page).astype(BF16)

    n_blocks = n_pages + 1
    z = jnp.concatenate(
        [lax.dot_general(q_bd, k_new, nt, preferred_element_type=F32) + bias]
        + [jnp.dot(q_bd, past(k_refs[j]), preferred_element_type=F32) + bias
           for j in reversed(range(n_pages))], axis=0)
    r = lax.broadcasted_iota(jnp.int32, (n_blocks * rows, page), 0)
    c = lax.broadcasted_iota(jnp.int32, (n_blocks * rows, page), 1)
    mask = (r >= rows) | (c < (r % tq))
    log_beta, neg_log_keep = _sb_log2_terms(z)
    neg_log_keep = jnp.where(mask, neg_log_keep, 0.0)
    later = _split_dot(neg_log_keep, tri)
    tot = jnp.sum(neg_log_keep, axis=-1, keepdims=True)
    run = jnp.zeros((rows, 1), F32)
    runs = []
    for b in range(n_blocks):
        runs.append(run)
        run = run + tot[b * rows:(b + 1) * rows]
    w = jnp.where(mask, jnp.exp2(log_beta - later - jnp.concatenate(runs, axis=0)), 0.0)

    w = w.astype(BF16)
    acc = jnp.dot(w[:rows], v_new, preferred_element_type=F32)
    for b in range(1, n_blocks):
        acc = acc + lax.dot_general(w[b * rows:(b + 1) * rows], past(v_refs[n_pages - b]), nt,
                                    preferred_element_type=F32)
    o_ref[...] = jnp.concatenate(
        [acc[h * tq:(h + 1) * tq, h * hd:(h + 1) * hd] for h in range(n_heads)],
        axis=-1).astype(o_ref.dtype)


def _attn_sample(page_table, bias, q, k, v, cache_k, cache_v, layer, tq):
    n_dec, n_pages = page_table.shape
    _, _, n_heads, hd, page = cache_k.shape
    aw = n_heads * hd

    def page_spec(j):
        return pl.BlockSpec((None, None, n_heads, hd, page),
                            lambda b, pt: (layer, pt[b * n_pages + j], 0, 0, 0))

    row = pl.BlockSpec((tq, aw), lambda b, pt: (b, 0))
    grid_spec = pltpu.PrefetchScalarGridSpec(
        num_scalar_prefetch=1,
        grid=(n_dec,),
        in_specs=[pl.BlockSpec(memory_space=pltpu.SMEM), row, row, row]
        + [page_spec(j) for j in range(n_pages)] * 2,
        out_specs=pl.BlockSpec((tq, aw), lambda b, pt: (b, 0)),
    )
    return pl.pallas_call(
        functools.partial(_attn_sample_kernel, n_pages=n_pages, n_heads=n_heads, page=page),
        grid_spec=grid_spec,
        out_shape=jax.ShapeDtypeStruct((n_dec * tq, aw), BF16),
        compiler_params=_cparams(("parallel",)),
        name="attn_sample",
    )(page_table.reshape(-1), bias, q, k, v, *([cache_k] * n_pages), *([cache_v] * n_pages))


def _discretize_kernel(are_ref, aim_ref, ldt_ref, bre_ref, bim_ref,
                       abre_ref, abim_ref, bbre_ref, bbim_ref):
    a_re = are_ref[...]
    a_im = aim_ref[...]
    dt = jnp.exp(ldt_ref[...])
    mag = jnp.exp(a_re * dt)
    ang = a_im * dt
    ab_re = mag * jnp.cos(ang)
    ab_im = mag * jnp.sin(ang)
    nr = ab_re - 1.0
    den = a_re * a_re + a_im * a_im
    c_re = (nr * a_re + ab_im * a_im) / den
    c_im = (ab_im * a_re - nr * a_im) / den
    b_re = bre_ref[...]
    b_im = bim_ref[...]
    abre_ref[...] = ab_re
    abim_ref[...] = ab_im
    bbre_ref[...] = c_re * b_re - c_im * b_im
    bbim_ref[...] = c_re * b_im + c_im * b_re


def _discretize(a_re, a_im, log_dt, b_re, b_im):
    depth, g, _, n = a_re.shape
    c = b_re.shape[2]
    sa = pl.BlockSpec((None, g, 1, n), lambda l: (l, 0, 0, 0))
    sb = pl.BlockSpec((None, g, c, n), lambda l: (l, 0, 0, 0))
    return pl.pallas_call(
        _discretize_kernel,
        grid=(depth,),
        in_specs=[sa, sa, pl.BlockSpec((None, g, 1, 1), lambda l: (l, 0, 0, 0)), sb, sb],
        out_specs=(sa, sa, sb, sb),
        out_shape=(jax.ShapeDtypeStruct((depth, g, 1, n), F32),) * 2
        + (jax.ShapeDtypeStruct((depth, g, c, n), F32),) * 2,
        compiler_params=_cparams(("parallel",)),
        name="ssm_discretize",
    )(a_re, a_im, log_dt, b_re, b_im)


def _ssm_kernel(u_ref, h0re_ref, h0im_ref, are_ref, aim_ref, bbd_ref, cre_ref, cim_ref,
                d_ref, wglu_ref, y_ref, hre_ref, him_ref, bu_ref, cre_s, cim_s, *, n_tiles, steps):
    tc = pl.program_id(1)
    ns = are_ref.shape[1]
    st = SSM_SEQ_TILE

    @pl.when(tc == 0)
    def _():
        cre_s[...] = h0re_ref[...]
        cim_s[...] = h0im_ref[...]

    u = u_ref[...]
    ub = u.astype(BF16)
    n_blocks, cin, _ = bbd_ref.shape
    cs = SSM_COL_TILE
    for cg in range(n_blocks):
        bu_ref[:, 2 * cs * cg:2 * cs * (cg + 1)] = jnp.dot(
            ub[:, cin * cg:cin * (cg + 1)], bbd_ref[cg], preferred_element_type=F32)

    for cg in range(n_blocks):
        nat = slice(cg * cs, (cg + 1) * cs)
        re = slice(2 * cs * cg, 2 * cs * cg + cs)
        im = slice(2 * cs * cg + cs, 2 * cs * (cg + 1))
        ar = jnp.broadcast_to(are_ref[:, nat], (st, cs))
        ai = jnp.broadcast_to(aim_ref[:, nat], (st, cs))

        def tile(bt, _):
            s0 = pl.multiple_of(bt * st, st)

            def step(t, carry):
                hr, hi = carry
                r0 = pl.multiple_of((t * n_tiles + bt) * st, st)
                nhr = ar * hr - ai * hi + bu_ref[pl.ds(r0, st), re]
                nhi = ar * hi + ai * hr + bu_ref[pl.ds(r0, st), im]
                bu_ref[pl.ds(r0, st), re] = nhr
                bu_ref[pl.ds(r0, st), im] = nhi
                return nhr, nhi

            hr, hi = lax.fori_loop(0, steps, step,
                                   (cre_s[pl.ds(s0, st), nat], cim_s[pl.ds(s0, st), nat]),
                                   unroll=8)
            cre_s[pl.ds(s0, st), nat] = hr
            cim_s[pl.ds(s0, st), nat] = hi
            return 0

        lax.fori_loop(0, n_tiles, tile, 0)

    y = jnp.concatenate(
        [_bdot(bu_ref[:, 2 * cs * cg:2 * cs * cg + cs], cre_ref[cg])
         - _bdot(bu_ref[:, 2 * cs * cg + cs:2 * cs * (cg + 1)], cim_ref[cg])
         for cg in range(n_blocks)], axis=-1) + d_ref[...] * u
    yg = jax.nn.gelu(y)
    y_ref[...] = (yg * jax.nn.sigmoid(_bdot(yg, wglu_ref[...]))).astype(y_ref.dtype)

    @pl.when(tc == pl.num_programs(1) - 1)
    def _():
        hre_ref[...] = cre_s[...]
        him_ref[...] = cim_s[...]


def _ssm(u_tb, h0_re, h0_im, ab_re, ab_im, bbd, cre_bd, cim_bd, d_skip, w_glu, n_tiles, steps):
    n, sw = u_tb.shape
    n_seq, ns = h0_re.shape
    seqs = n_tiles * SSM_SEQ_TILE
    rows = steps * seqs
    seq_blocks = n_seq // seqs
    time_blocks = n // (rows * seq_blocks)
    assert seq_blocks == 1 or time_blocks == 1
    row_map = (lambda s, t: (t, 0)) if seq_blocks == 1 else (lambda s, t: (s, 0))
    state = pl.BlockSpec((seqs, ns), lambda s, t: (s, 0))
    return pl.pallas_call(
        functools.partial(_ssm_kernel, n_tiles=n_tiles, steps=steps),
        grid=(seq_blocks, time_blocks),
        in_specs=[
            pl.BlockSpec((rows, sw), row_map), state, state,
            _const_spec((1, ns)), _const_spec((1, ns)),
            _const_spec(bbd.shape), _const_spec(cre_bd.shape), _const_spec(cim_bd.shape),
            _const_spec((1, sw)), _const_spec((sw, sw)),
        ],
        out_specs=(pl.BlockSpec((rows, sw), row_map), state, state),
        out_shape=(jax.ShapeDtypeStruct((n, sw), BF16),
                   jax.ShapeDtypeStruct((n_seq, ns), F32),
                   jax.ShapeDtypeStruct((n_seq, ns), F32)),
        scratch_shapes=[pltpu.VMEM((rows, 2 * ns), F32),
                        pltpu.VMEM((seqs, ns), F32), pltpu.VMEM((seqs, ns), F32)],
        compiler_params=_cparams(("parallel", "arbitrary")),
        name="ssm",
    )(u_tb, h0_re, h0_im, ab_re, ab_im, bbd, cre_bd, cim_bd, d_skip, w_glu)


def _merge_kernel(op_ref, os_ref, yp_ref, ys_ref, xp_ref, xs_ref, g_ref, wg_ref, wa_ref, ws_ref,
                  wo_ref, outp_ref, outs_ref, *, prompt_tiles):
    is_prompt = pl.program_id(0) < prompt_tiles
    x = _read_pair(is_prompt, xp_ref, xs_ref)
    d = x.shape[1]
    h = _rmsnorm(x, g_ref[...]).astype(BF16)
    g_att = jnp.dot(h, wg_ref[:, :d], preferred_element_type=F32)
    g_ssm = jnp.dot(h, wg_ref[:, d:], preferred_element_type=F32)
    att = jnp.dot(_read_pair(is_prompt, op_ref, os_ref), wa_ref[...], preferred_element_type=F32)
    ssm = jnp.dot(_read_pair(is_prompt, yp_ref, ys_ref), ws_ref[...], preferred_element_type=F32)
    merged = jax.nn.sigmoid(g_att) * att + jax.nn.sigmoid(g_ssm) * ssm
    _write_pair(is_prompt, outp_ref, outs_ref, x + _bdot(merged, wo_ref[...]))


def _merge(o_p, o_s, y_p, y_s, x_p, x_s, g, w_gates, w_attn_up, w_ssm_up, w_out):
    n_p, d = x_p.shape
    n_s = x_s.shape[0]
    aw = o_p.shape[1]
    sw = y_p.shape[1]
    pt = n_p // ROW_TILE
    return pl.pallas_call(
        functools.partial(_merge_kernel, prompt_tiles=pt),
        grid=(pt + n_s // ROW_TILE,),
        in_specs=_pair_specs(aw, pt) + _pair_specs(sw, pt) + _pair_specs(d, pt)
        + [_const_spec((1, d)), _const_spec((d, 2 * d)),
           _const_spec((aw, d)), _const_spec((sw, d)), _const_spec((d, d))],
        out_specs=_pair_specs(d, pt),
        out_shape=_pair_shapes(n_p, n_s, d, F32),
        compiler_params=_cparams(("arbitrary",)),
        name="merge",
    )(o_p, o_s, y_p, y_s, x_p, x_s, g, w_gates, w_attn_up, w_ssm_up, w_out)


def _ffn_chunks(d_ff):
    chunk = 768
    edges = list(range(0, d_ff, chunk)) + [d_ff]
    return list(zip(edges[:-1], edges[1:]))


def _ffn_kernel(xp_ref, xs_ref, g_ref, wg_ref, wu_ref, wd_ref, *rest, prompt_tiles, final):
    outp_ref, outs_ref = rest[-2:]
    is_prompt = pl.program_id(0) < prompt_tiles
    x = _read_pair(is_prompt, xp_ref, xs_ref)
    h = _rmsnorm(x, g_ref[...]).astype(BF16)
    acc = x
    for c0, c1 in _ffn_chunks(wg_ref.shape[1]):
        gate = jnp.dot(h, wg_ref[:, c0:c1], preferred_element_type=F32)
        up = jnp.dot(h, wu_ref[:, c0:c1], preferred_element_type=F32)
        acc = acc + _bdot(jax.nn.silu(gate) * up, wd_ref[c0:c1, :])
    if final:
        acc = _rmsnorm(acc, rest[0][...])
    _write_pair(is_prompt, outp_ref, outs_ref, acc)


def _ffn(x_p, x_s, g, w_gate, w_up, w_down, final_g=None):
    n_p, d = x_p.shape
    n_s = x_s.shape[0]
    d_ff = w_gate.shape[1]
    pt = n_p // ROW_TILE
    final = final_g is not None
    in_specs = _pair_specs(d, pt) + [_const_spec((1, d)), _const_spec((d, d_ff)),
                                     _const_spec((d, d_ff)), _const_spec((d_ff, d))]
    args = [x_p, x_s, g, w_gate, w_up, w_down]
    if final:
        in_specs.append(_const_spec((1, d)))
        args.append(final_g)
    return pl.pallas_call(
        functools.partial(_ffn_kernel, prompt_tiles=pt, final=final),
        grid=(pt + n_s // ROW_TILE,),
        in_specs=in_specs,
        out_specs=_pair_specs(d, pt),
        out_shape=_pair_shapes(n_p, n_s, d, F32),
        compiler_params=_cparams(("arbitrary",)),
        name="ffn",
    )(*args)


def _block_diag(m):
    g, r, c = m.shape
    eye = jnp.eye(g, dtype=m.dtype)
    return (eye[:, None, :, None] * m[:, :, None, :]).reshape(g * r, g * c)


def _time_major(a, n_seq, seq):
    return a.reshape(n_seq, seq, -1).transpose(1, 0, 2).reshape(n_seq * seq, -1)


def _seq_major(a, n_seq, seq):
    return a.reshape(seq, n_seq, -1).transpose(1, 0, 2).reshape(n_seq * seq, -1)


def kernel(x_prompt, x_sample, cache_k, cache_v, state_ssm_re, state_ssm_im, page_table, norm1_g, w_in, sb_bias, ssm_a_re, ssm_a_im, ssm_log_dt, ssm_b_re, ssm_b_im, ssm_c_re, ssm_c_im, ssm_d, ssm_w_glu, w_attn_up, w_ssm_up, w_out, norm2_g, w_ffn_gate, w_ffn_up, w_ffn_down, final_norm_g):
    n_seq_p, seq_p, d = x_prompt.shape
    n_seq_s, seq_s, _ = x_sample.shape
    depth = w_in.shape[0]
    _, groups, n_state = ssm_a_re.shape
    aw = w_attn_up.shape[1]
    sw = w_ssm_up.shape[1]
    n_heads = aw // HEAD_DIM
    ns = groups * n_state
    n_p = n_seq_p * seq_p
    n_s = n_seq_s * seq_s

    x_p = x_prompt.reshape(n_p, d)
    x_s = x_sample.reshape(n_s, d)

    ab_re, ab_im, bb_re, bb_im = _discretize(
        ssm_a_re[:, :, None, :], ssm_a_im[:, :, None, :], ssm_log_dt[:, :, None, None],
        ssm_b_re.transpose(0, 1, 3, 2), ssm_b_im.transpose(0, 1, 3, 2))

    cache_kt = cache_k.transpose(0, 1, 3, 4, 2)
    cache_vt = cache_v.transpose(0, 1, 3, 4, 2)

    gpb = SSM_COL_TILE // n_state
    n_cb = groups // gpb
    blocks = lambda m: jax.vmap(_block_diag)(m.reshape(n_cb, gpb, *m.shape[1:]))

    zeros_p = jnp.zeros((n_seq_p, ns), F32)
    outs = [[] for _ in range(6)]
    kv_t = None
    for l in range(depth):
        bbd = jnp.concatenate([blocks(bb_re[l]), blocks(bb_im[l])], axis=2).astype(BF16)
        cre_bd = blocks(ssm_c_re[l].transpose(0, 2, 1)).astype(BF16)
        cim_bd = blocks(ssm_c_im[l].transpose(0, 2, 1)).astype(BF16)
        a_re_l = ab_re[l].reshape(1, ns)
        a_im_l = ab_im[l].reshape(1, ns)
        d_l = ssm_d[l].reshape(1, sw)
        wglu_l = ssm_w_glu[l].astype(BF16)

        w_in_l = w_in[l].astype(BF16)
        n_qkvu = 3 * aw + sw
        g1 = norm1_g[l].reshape(1, d)
        q_p, q_s, u_p, u_s, k_s, v_s, kt, vt = _inproj(
            x_p, x_s, g1, w_in_l[:, :n_qkvu], w_in_l[:, aw:3 * aw].T, kv_t,
            l, depth, n_seq_p, aw, sw)
        kv_t = (kt, vt)

        o_p = _attn_prompt(sb_bias[l], q_p, kt, vt, l)
        o_s = _attn_sample(page_table, sb_bias[l], q_s, k_s, v_s, cache_kt, cache_vt, l, seq_s)

        ssm_w = (a_re_l, a_im_l, bbd, cre_bd, cim_bd, d_l, wglu_l)
        y_p, hre_p, him_p = _ssm(_time_major(u_p, n_seq_p, seq_p), zeros_p, zeros_p, *ssm_w,
                                 n_tiles=n_seq_p // SSM_SEQ_TILE, steps=SSM_TIME_TILE)
        y_s, hre_s, him_s = _ssm(_time_major(u_s, n_seq_s, seq_s),
                                 state_ssm_re[l].reshape(n_seq_s, ns),
                                 state_ssm_im[l].reshape(n_seq_s, ns), *ssm_w,
                                 n_tiles=n_seq_s // SSM_SEQ_TILE, steps=seq_s)
        x_p, x_s = _merge(o_p, o_s, _seq_major(y_p, n_seq_p, seq_p), _seq_major(y_s, n_seq_s, seq_s),
                          x_p, x_s, g1, w_in_l[:, n_qkvu:], w_attn_up[l].astype(BF16),
                          w_ssm_up[l].astype(BF16), w_out[l].astype(BF16))
        x_p, x_s = _ffn(x_p, x_s, norm2_g[l].reshape(1, d), w_ffn_gate[l].astype(BF16),
                        w_ffn_up[l].astype(BF16), w_ffn_down[l].astype(BF16),
                        final_norm_g.reshape(1, d) if l == depth - 1 else None)

        for dst, val in zip(outs, (k_s, v_s, hre_p, him_p, hre_s, him_s)):
            dst.append(val)

    ks, vs, hrp, hip, hrs, his = (jnp.stack(o) for o in outs)
    heads_last = lambda a: a.reshape(depth, n_seq_p, n_heads, HEAD_DIM, seq_p).transpose(0, 1, 4, 2, 3)
    return (x_p.reshape(n_seq_p, seq_p, d),
            x_s.reshape(n_seq_s, seq_s, d),
            heads_last(kv_t[0]),
            heads_last(kv_t[1]),
            ks.reshape(depth, n_seq_s, seq_s, n_heads, HEAD_DIM),
            vs.reshape(depth, n_seq_s, seq_s, n_heads, HEAD_DIM),
            hrp.reshape(depth, n_seq_p, groups, n_state),
            hip.reshape(depth, n_seq_p, groups, n_state),
            hrs.reshape(depth, n_seq_s, groups, n_state),
            his.reshape(depth, n_seq_s, groups, n_state))
```

```python
import functools
import math

import jax
import jax.numpy as jnp
from jax import lax
from jax.experimental import pallas as pl
from jax.experimental.pallas import tpu as pltpu

F32 = jnp.float32
BF16 = jnp.bfloat16

HEAD_DIM = 64
SSM_GROUP_CH = 16
RMS_EPS = 1e-6
LOG2_E = 1.0 / math.log(2.0)
Q_SCALE = HEAD_DIM ** -0.5 * LOG2_E
ROW_TILE = 512
ATT_BLOCK = 256
ATT_HEADS = 8
ATT_SEQS = 2
SSM_SEQ_TILE = 8
SSM_TIME_TILE = 64
SSM_COL_TILE = 512
VMEM_LIMIT = 56 * 1024 * 1024


def _cparams(sem):
    return pltpu.CompilerParams(dimension_semantics=sem, vmem_limit_bytes=VMEM_LIMIT)


def _const_spec(shape):
    nd = len(shape)
    return pl.BlockSpec(shape, lambda *_: (0,) * nd)


def _rmsnorm(x, g):
    ms = jnp.mean(x * x, axis=-1, keepdims=True)
    return x * lax.rsqrt(ms + RMS_EPS) * g


def _bdot(a, b):
    return jnp.dot(a.astype(BF16), b.astype(BF16), preferred_element_type=F32)


def _sb_log2_terms(t):
    log_beta = jnp.minimum(t, 0.0) - jnp.log2(1.0 + jnp.exp2(-jnp.abs(t)))
    return log_beta, t - log_beta


def _split_dot(x, m):
    hi = x.astype(BF16)
    lo = (x - hi.astype(F32)).astype(BF16)
    return (jnp.dot(hi, m, preferred_element_type=F32)
            + jnp.dot(lo, m, preferred_element_type=F32))


def _pair_specs(width, prompt_tiles):
    prompt = pl.BlockSpec((ROW_TILE, width), lambda i: (jnp.minimum(i, prompt_tiles - 1), 0))
    sample = pl.BlockSpec((ROW_TILE, width), lambda i: (jnp.maximum(i - prompt_tiles, 0), 0))
    return [prompt, sample]


def _pair_shapes(n_p, n_s, width, dtype):
    return [jax.ShapeDtypeStruct((n_p, width), dtype), jax.ShapeDtypeStruct((n_s, width), dtype)]


def _read_pair(is_prompt, p_ref, s_ref):
    return jnp.where(is_prompt, p_ref[...], s_ref[...])


def _write_pair(is_prompt, p_ref, s_ref, value):
    @pl.when(is_prompt)
    def _():
        p_ref[...] = value

    @pl.when(jnp.logical_not(is_prompt))
    def _():
        s_ref[...] = value


def _inproj_kernel(xp_ref, xs_ref, g_ref, w_ref, wt_ref, *refs, aw, sw, prompt_tiles, aliased):
    (qp_ref, qs_ref, up_ref, us_ref, ks_ref, vs_ref, kt_ref, vt_ref) = refs[2 if aliased else 0:]
    is_prompt = pl.program_id(0) < prompt_tiles
    h = _rmsnorm(_read_pair(is_prompt, xp_ref, xs_ref), g_ref[...]).astype(BF16)

    def proj(c0, n):
        return jnp.dot(h, w_ref[:, c0:c0 + n], preferred_element_type=F32)

    _write_pair(is_prompt, qp_ref, qs_ref, proj(0, aw) * Q_SCALE)
    _write_pair(is_prompt, up_ref, us_ref, proj(3 * aw, sw))

    @pl.when(is_prompt)
    def _():
        nt = (((1,), (1,)), ((), ()))
        kv = lax.dot_general(wt_ref[...], h, nt, preferred_element_type=F32)
        kt_ref[...] = kv[:aw]
        vt_ref[...] = kv[aw:]

    @pl.when(jnp.logical_not(is_prompt))
    def _():
        ks_ref[...] = proj(aw, aw)
        vs_ref[...] = proj(2 * aw, aw)


def _inproj(x_p, x_s, g, w_qkvu, w_kv_t, kv_t, layer, depth, n_seq_p, aw, sw):
    n_p, d = x_p.shape
    n_s = x_s.shape[0]
    seq_p = n_p // n_seq_p
    seq_tiles = seq_p // ROW_TILE
    pt = n_p // ROW_TILE
    aliased = kv_t is not None

    def kt_map(i):
        j = jnp.minimum(i, pt - 1)
        return (layer, j // seq_tiles, 0, j % seq_tiles)

    kt_spec = pl.BlockSpec((None, None, aw, ROW_TILE), kt_map)
    stacked = jax.ShapeDtypeStruct((depth, n_seq_p, aw, seq_p), F32)
    sample_kv = _pair_specs(aw, pt)[1]
    any_spec = pl.BlockSpec(memory_space=pl.ANY)
    return pl.pallas_call(
        functools.partial(_inproj_kernel, aw=aw, sw=sw, prompt_tiles=pt, aliased=aliased),
        grid=(pt + n_s // ROW_TILE,),
        in_specs=_pair_specs(d, pt)
        + [_const_spec((1, d)), _const_spec(w_qkvu.shape), _const_spec((2 * aw, d))]
        + ([any_spec, any_spec] if aliased else []),
        out_specs=_pair_specs(aw, pt) + _pair_specs(sw, pt) + [sample_kv, sample_kv, kt_spec, kt_spec],
        out_shape=_pair_shapes(n_p, n_s, aw, F32)
        + _pair_shapes(n_p, n_s, sw, F32)
        + _pair_shapes(n_p, n_s, aw, F32)[1:] * 2
        + [stacked, stacked],
        input_output_aliases={5: 6, 6: 7} if aliased else {},
        compiler_params=_cparams(("arbitrary",)),
        name="inproj",
    )(x_p, x_s, g, w_qkvu, w_kv_t, *(kv_t if aliased else ()))


def _later_key_matrix(n):
    r = lax.broadcasted_iota(jnp.int32, (n, n), 0)
    c = lax.broadcasted_iota(jnp.int32, (n, n), 1)
    return (r > c).astype(BF16)


def _attn_prompt_kernel(bias_ref, q_ref, kt_ref, vt_ref, o_ref, acc_ref, run_ref):
    grp = pl.program_id(1)
    qi = pl.program_id(2)
    blk = ATT_BLOCK
    pair = 2 * HEAD_DIM
    n_pairs = ATT_HEADS // 2
    lane = lax.broadcasted_iota(jnp.int32, (blk, pair), 1)
    in_head = (lane < HEAD_DIM, lane >= HEAD_DIM)
    feat = lax.broadcasted_iota(jnp.int32, (pair, blk), 0)
    v_rows = (feat < HEAD_DIM, feat >= HEAD_DIM)
    tri = _later_key_matrix(blk)
    r = lax.broadcasted_iota(jnp.int32, (blk, blk), 0)
    c = lax.broadcasted_iota(jnp.int32, (blk, blk), 1)
    causal = c < r
    nt = (((1,), (1,)), ((), ()))

    q = q_ref[...].astype(BF16)
    qh = [jnp.where(in_head[h % 2], q[:, (h // 2) * pair:(h // 2 + 1) * pair], 0).astype(BF16)
          for h in range(ATT_HEADS)]
    bias = [bias_ref[grp * ATT_HEADS + h] * LOG2_E for h in range(ATT_HEADS)]
    acc_ref[...] = jnp.zeros_like(acc_ref)
    run_ref[...] = jnp.zeros_like(run_ref)

    def block(j, mask):
        start = pl.multiple_of(j * blk, blk)
        heads = range(ATT_HEADS)
        kb = [kt_ref[p * pair:(p + 1) * pair, pl.ds(start, blk)].astype(BF16)
              for p in range(n_pairs)]
        vb = [vt_ref[p * pair:(p + 1) * pair, pl.ds(start, blk)].astype(BF16)
              for p in range(n_pairs)]
        z = [jnp.dot(qh[h], kb[h // 2], preferred_element_type=F32) for h in heads]
        terms = [_sb_log2_terms(z[h] + bias[h]) for h in heads]
        log_beta = [t[0] for t in terms]
        keep = [t[1] if mask is None else jnp.where(mask, t[1], 0.0) for t in terms]
        later = [jnp.dot(keep[h].astype(BF16), tri, preferred_element_type=F32) for h in heads]
        w = [jnp.exp2(log_beta[h] - (later[h] + run_ref[h])) for h in heads]
        if mask is not None:
            w = [jnp.where(mask, w[h], 0.0) for h in heads]
        for h in heads:
            run_ref[h] = run_ref[h] + jnp.sum(keep[h], axis=-1, keepdims=True)
        for p in range(n_pairs):
            pv = [lax.dot_general(w[2 * p + hh].astype(BF16),
                                  jnp.where(v_rows[hh], vb[p], 0).astype(BF16), nt,
                                  preferred_element_type=F32) for hh in range(2)]
            acc_ref[p] = acc_ref[p] + (pv[0] + pv[1])

    block(qi, causal)

    def body(i, carry):
        block(qi - 1 - i, None)
        return carry

    lax.fori_loop(0, qi, body, 0)
    o_ref[...] = jnp.concatenate([acc_ref[p] for p in range(n_pairs)], axis=-1).astype(o_ref.dtype)


def _attn_prompt(bias, q, kt, vt, layer):
    _, n_seq, aw, seq = kt.shape
    blk = ATT_BLOCK
    nq = seq // blk
    width = ATT_HEADS * HEAD_DIM
    kv_spec = pl.BlockSpec((None, None, width, seq), lambda b, g, i: (layer, b, g, 0))
    return pl.pallas_call(
        _attn_prompt_kernel,
        grid=(n_seq, aw // width, nq),
        in_specs=[
            pl.BlockSpec(memory_space=pltpu.SMEM),
            pl.BlockSpec((blk, width), lambda b, g, i: (b * nq + i, g)),
            kv_spec, kv_spec,
        ],
        out_specs=pl.BlockSpec((blk, width), lambda b, g, i: (b * nq + i, g)),
        out_shape=jax.ShapeDtypeStruct((n_seq * seq, aw), BF16),
        scratch_shapes=[pltpu.VMEM((ATT_HEADS // 2, blk, 2 * HEAD_DIM), F32),
                        pltpu.VMEM((ATT_HEADS, blk, 1), F32)],
        compiler_params=_cparams(("parallel", "parallel", "arbitrary")),
        name="attn_prompt",
    )(bias, q, kt, vt)


def _attn_sample_kernel(pt_ref, bias_ref, q_ref, kn_ref, vn_ref, *refs, n_pages, n_heads, page, tq):
    del pt_ref
    n_page_refs = ATT_SEQS * n_pages
    k_refs = refs[:n_page_refs]
    v_refs = refs[n_page_refs:2 * n_page_refs]
    o_ref = refs[2 * n_page_refs]
    hd = HEAD_DIM
    aw = n_heads * hd
    rows = n_heads * tq
    nt = (((1,), (1,)), ((), ()))
    seqs = range(ATT_SEQS)

    row_head = lax.broadcasted_iota(jnp.int32, (rows, aw), 0) // tq
    col_head = lax.broadcasted_iota(jnp.int32, (rows, aw), 1) // hd
    on_diag = row_head == col_head
    q_bd = [jnp.where(on_diag, jnp.concatenate([q_ref[s * tq:(s + 1) * tq, :]] * n_heads, axis=0),
                      0.0).astype(BF16) for s in seqs]
    bias_head = lax.broadcasted_iota(jnp.int32, (rows, 1), 0) // tq
    bias = jnp.zeros((rows, 1), F32)
    for h in range(n_heads):
        bias = jnp.where(bias_head == h, bias_ref[h] * LOG2_E, bias)

    pad = jnp.zeros((page - tq, aw), F32)

    def new(ref, s):
        return jnp.concatenate([ref[s * tq:(s + 1) * tq, :], pad], axis=0).astype(BF16)

    def past(ref):
        return ref[...].reshape(aw, page).astype(BF16)

    tri = _later_key_matrix(page)

    n_blocks = n_pages + 1
    z = jnp.concatenate(
        [blk for s in seqs for blk in
         [lax.dot_general(q_bd[s], new(kn_ref, s), nt, preferred_element_type=F32) + bias]
         + [jnp.dot(q_bd[s], past(k_refs[s * n_pages + j]), preferred_element_type=F32) + bias
            for j in reversed(range(n_pages))]], axis=0)
    r = lax.broadcasted_iota(jnp.int32, (ATT_SEQS * n_blocks * rows, page), 0)
    c = lax.broadcasted_iota(jnp.int32, (ATT_SEQS * n_blocks * rows, page), 1)
    mask = ((r % (n_blocks * rows)) >= rows) | (c < (r % tq))
    log_beta, neg_log_keep = _sb_log2_terms(z)
    neg_log_keep = jnp.where(mask, neg_log_keep, 0.0)
    later = _split_dot(neg_log_keep, tri)
    tot = jnp.sum(neg_log_keep, axis=-1, keepdims=True)
    runs = []
    for s in seqs:
        run = jnp.zeros((rows, 1), F32)
        for b in range(n_blocks):
            runs.append(run)
            r0 = (s * n_blocks + b) * rows
            run = run + tot[r0:r0 + rows]
    w = jnp.where(mask, jnp.exp2(log_beta - later - jnp.concatenate(runs, axis=0)), 0.0)
    w = w.astype(BF16)

    outs = []
    for s in seqs:
        r0 = s * n_blocks * rows
        acc = jnp.dot(w[r0:r0 + rows], new(vn_ref, s), preferred_element_type=F32)
        for b in range(1, n_blocks):
            acc = acc + lax.dot_general(w[r0 + b * rows:r0 + (b + 1) * rows],
                                        past(v_refs[s * n_pages + n_pages - b]), nt,
                                        preferred_element_type=F32)
        outs.append(jnp.concatenate(
            [acc[h * tq:(h + 1) * tq, h * hd:(h + 1) * hd] for h in range(n_heads)], axis=-1))
    o_ref[...] = jnp.concatenate(outs, axis=0).astype(o_ref.dtype)


def _attn_sample(page_table, bias, q, k, v, cache_k, cache_v, layer, tq):
    n_dec, n_pages = page_table.shape
    _, _, n_heads, hd, page = cache_k.shape
    aw = n_heads * hd

    def page_spec(s, j):
        return pl.BlockSpec((None, None, n_heads, hd, page),
                            lambda b, pt: (layer, pt[(b * ATT_SEQS + s) * n_pages + j], 0, 0, 0))

    pages = [page_spec(s, j) for s in range(ATT_SEQS) for j in range(n_pages)]
    row = pl.BlockSpec((ATT_SEQS * tq, aw), lambda b, pt: (b, 0))
    grid_spec = pltpu.PrefetchScalarGridSpec(
        num_scalar_prefetch=1,
        grid=(n_dec // ATT_SEQS,),
        in_specs=[pl.BlockSpec(memory_space=pltpu.SMEM), row, row, row] + pages * 2,
        out_specs=row,
    )
    return pl.pallas_call(
        functools.partial(_attn_sample_kernel, n_pages=n_pages, n_heads=n_heads, page=page, tq=tq),
        grid_spec=grid_spec,
        out_shape=jax.ShapeDtypeStruct((n_dec * tq, aw), BF16),
        compiler_params=_cparams(("parallel",)),
        name="attn_sample",
    )(page_table.reshape(-1), bias, q, k, v, *([cache_k] * len(pages)), *([cache_v] * len(pages)))


def _discretize_kernel(are_ref, aim_ref, ldt_ref, bre_ref, bim_ref,
                       abre_ref, abim_ref, bbre_ref, bbim_ref):
    a_re = are_ref[...]
    a_im = aim_ref[...]
    dt = jnp.exp(ldt_ref[...])
    mag = jnp.exp(a_re * dt)
    ang = a_im * dt
    ab_re = mag * jnp.cos(ang)
    ab_im = mag * jnp.sin(ang)
    nr = ab_re - 1.0
    den = a_re * a_re + a_im * a_im
    c_re = (nr * a_re + ab_im * a_im) / den
    c_im = (ab_im * a_re - nr * a_im) / den
    b_re = bre_ref[...]
    b_im = bim_ref[...]
    abre_ref[...] = ab_re
    abim_ref[...] = ab_im
    bbre_ref[...] = c_re * b_re - c_im * b_im
    bbim_ref[...] = c_re * b_im + c_im * b_re


def _discretize(a_re, a_im, log_dt, b_re, b_im):
    depth, g, _, n = a_re.shape
    c = b_re.shape[2]
    sa = pl.BlockSpec((None, g, 1, n), lambda l: (l, 0, 0, 0))
    sb = pl.BlockSpec((None, g, c, n), lambda l: (l, 0, 0, 0))
    return pl.pallas_call(
        _discretize_kernel,
        grid=(depth,),
        in_specs=[sa, sa, pl.BlockSpec((None, g, 1, 1), lambda l: (l, 0, 0, 0)), sb, sb],
        out_specs=(sa, sa, sb, sb),
        out_shape=(jax.ShapeDtypeStruct((depth, g, 1, n), F32),) * 2
        + (jax.ShapeDtypeStruct((depth, g, c, n), F32),) * 2,
        compiler_params=_cparams(("parallel",)),
        name="ssm_discretize",
    )(a_re, a_im, log_dt, b_re, b_im)


def _ssm_kernel(u_ref, h0re_ref, h0im_ref, are_ref, aim_ref, bbd_ref, cre_ref, cim_ref,
                d_ref, wglu_ref, y_ref, hre_ref, him_ref, bu_ref, cre_s, cim_s, *, n_tiles, steps):
    tc = pl.program_id(1)
    ns = are_ref.shape[1]
    st = SSM_SEQ_TILE

    @pl.when(tc == 0)
    def _():
        cre_s[...] = h0re_ref[...]
        cim_s[...] = h0im_ref[...]

    u = u_ref[...]
    ub = u.astype(BF16)
    n_blocks, cin, _ = bbd_ref.shape
    cs = SSM_COL_TILE
    for cg in range(n_blocks):
        bu_ref[:, 2 * cs * cg:2 * cs * (cg + 1)] = jnp.dot(
            ub[:, cin * cg:cin * (cg + 1)], bbd_ref[cg], preferred_element_type=F32)

    for cg in range(n_blocks):
        nat = slice(cg * cs, (cg + 1) * cs)
        re = slice(2 * cs * cg, 2 * cs * cg + cs)
        im = slice(2 * cs * cg + cs, 2 * cs * (cg + 1))
        ar = jnp.broadcast_to(are_ref[:, nat], (st, cs))
        ai = jnp.broadcast_to(aim_ref[:, nat], (st, cs))

        def tile(bt, _):
            s0 = pl.multiple_of(bt * st, st)

            def step(t, carry):
                hr, hi = carry
                r0 = pl.multiple_of((t * n_tiles + bt) * st, st)
                nhr = ar * hr - ai * hi + bu_ref[pl.ds(r0, st), re]
                nhi = ar * hi + ai * hr + bu_ref[pl.ds(r0, st), im]
                bu_ref[pl.ds(r0, st), re] = nhr
                bu_ref[pl.ds(r0, st), im] = nhi
                return nhr, nhi

            hr, hi = lax.fori_loop(0, steps, step,
                                   (cre_s[pl.ds(s0, st), nat], cim_s[pl.ds(s0, st), nat]),
                                   unroll=8)
            cre_s[pl.ds(s0, st), nat] = hr
            cim_s[pl.ds(s0, st), nat] = hi
            return 0

        lax.fori_loop(0, n_tiles, tile, 0)

    y = jnp.concatenate(
        [_bdot(bu_ref[:, 2 * cs * cg:2 * cs * cg + cs], cre_ref[cg])
         - _bdot(bu_ref[:, 2 * cs * cg + cs:2 * cs * (cg + 1)], cim_ref[cg])
         for cg in range(n_blocks)], axis=-1) + d_ref[...] * u
    yg = jax.nn.gelu(y)
    y_ref[...] = (yg * jax.nn.sigmoid(_bdot(yg, wglu_ref[...]))).astype(y_ref.dtype)

    @pl.when(tc == pl.num_programs(1) - 1)
    def _():
        hre_ref[...] = cre_s[...]
        him_ref[...] = cim_s[...]


def _ssm(u_tb, h0_re, h0_im, ab_re, ab_im, bbd, cre_bd, cim_bd, d_skip, w_glu, n_tiles, steps):
    n, sw = u_tb.shape
    n_seq, ns = h0_re.shape
    seqs = n_tiles * SSM_SEQ_TILE
    rows = steps * seqs
    seq_blocks = n_seq // seqs
    time_blocks = n // (rows * seq_blocks)
    assert seq_blocks == 1 or time_blocks == 1
    row_map = (lambda s, t: (t, 0)) if seq_blocks == 1 else (lambda s, t: (s, 0))
    state = pl.BlockSpec((seqs, ns), lambda s, t: (s, 0))
    return pl.pallas_call(
        functools.partial(_ssm_kernel, n_tiles=n_tiles, steps=steps),
        grid=(seq_blocks, time_blocks),
        in_specs=[
            pl.BlockSpec((rows, sw), row_map), state, state,
            _const_spec((1, ns)), _const_spec((1, ns)),
            _const_spec(bbd.shape), _const_spec(cre_bd.shape), _const_spec(cim_bd.shape),
            _const_spec((1, sw)), _const_spec((sw, sw)),
        ],
        out_specs=(pl.BlockSpec((rows, sw), row_map), state, state),
        out_shape=(jax.ShapeDtypeStruct((n, sw), BF16),
                   jax.ShapeDtypeStruct((n_seq, ns), F32),
                   jax.ShapeDtypeStruct((n_seq, ns), F32)),
        scratch_shapes=[pltpu.VMEM((rows, 2 * ns), F32),
                        pltpu.VMEM((seqs, ns), F32), pltpu.VMEM((seqs, ns), F32)],
        compiler_params=_cparams(("parallel", "arbitrary")),
        name="ssm",
    )(u_tb, h0_re, h0_im, ab_re, ab_im, bbd, cre_bd, cim_bd, d_skip, w_glu)


def _merge_kernel(op_ref, os_ref, yp_ref, ys_ref, xp_ref, xs_ref, g_ref, wg_ref, wa_ref, ws_ref,
                  wo_ref, outp_ref, outs_ref, *, prompt_tiles):
    is_prompt = pl.program_id(0) < prompt_tiles
    x = _read_pair(is_prompt, xp_ref, xs_ref)
    d = x.shape[1]
    h = _rmsnorm(x, g_ref[...]).astype(BF16)
    g_att = jnp.dot(h, wg_ref[:, :d], preferred_element_type=F32)
    g_ssm = jnp.dot(h, wg_ref[:, d:], preferred_element_type=F32)
    att = jnp.dot(_read_pair(is_prompt, op_ref, os_ref), wa_ref[...], preferred_element_type=F32)
    ssm = jnp.dot(_read_pair(is_prompt, yp_ref, ys_ref), ws_ref[...], preferred_element_type=F32)
    merged = jax.nn.sigmoid(g_att) * att + jax.nn.sigmoid(g_ssm) * ssm
    _write_pair(is_prompt, outp_ref, outs_ref, x + _bdot(merged, wo_ref[...]))


def _merge(o_p, o_s, y_p, y_s, x_p, x_s, g, w_gates, w_attn_up, w_ssm_up, w_out):
    n_p, d = x_p.shape
    n_s = x_s.shape[0]
    aw = o_p.shape[1]
    sw = y_p.shape[1]
    pt = n_p // ROW_TILE
    return pl.pallas_call(
        functools.partial(_merge_kernel, prompt_tiles=pt),
        grid=(pt + n_s // ROW_TILE,),
        in_specs=_pair_specs(aw, pt) + _pair_specs(sw, pt) + _pair_specs(d, pt)
        + [_const_spec((1, d)), _const_spec((d, 2 * d)),
           _const_spec((aw, d)), _const_spec((sw, d)), _const_spec((d, d))],
        out_specs=_pair_specs(d, pt),
        out_shape=_pair_shapes(n_p, n_s, d, F32),
        compiler_params=_cparams(("arbitrary",)),
        name="merge",
    )(o_p, o_s, y_p, y_s, x_p, x_s, g, w_gates, w_attn_up, w_ssm_up, w_out)


def _ffn_chunks(d_ff):
    chunk = 768
    edges = list(range(0, d_ff, chunk)) + [d_ff]
    return list(zip(edges[:-1], edges[1:]))


def _ffn_kernel(xp_ref, xs_ref, g_ref, wg_ref, wu_ref, wd_ref, *rest, prompt_tiles, final):
    outp_ref, outs_ref = rest[-2:]
    is_prompt = pl.program_id(0) < prompt_tiles
    x = _read_pair(is_prompt, xp_ref, xs_ref)
    h = _rmsnorm(x, g_ref[...]).astype(BF16)
    acc = x
    for c0, c1 in _ffn_chunks(wg_ref.shape[1]):
        gate = jnp.dot(h, wg_ref[:, c0:c1], preferred_element_type=F32)
        up = jnp.dot(h, wu_ref[:, c0:c1], preferred_element_type=F32)
        acc = acc + _bdot(jax.nn.silu(gate) * up, wd_ref[c0:c1, :])
    if final:
        acc = _rmsnorm(acc, rest[0][...])
    _write_pair(is_prompt, outp_ref, outs_ref, acc)


def _ffn(x_p, x_s, g, w_gate, w_up, w_down, final_g=None):
    n_p, d = x_p.shape
    n_s = x_s.shape[0]
    d_ff = w_gate.shape[1]
    pt = n_p // ROW_TILE
    final = final_g is not None
    in_specs = _pair_specs(d, pt) + [_const_spec((1, d)), _const_spec((d, d_ff)),
                                     _const_spec((d, d_ff)), _const_spec((d_ff, d))]
    args = [x_p, x_s, g, w_gate, w_up, w_down]
    if final:
        in_specs.append(_const_spec((1, d)))
        args.append(final_g)
    return pl.pallas_call(
        functools.partial(_ffn_kernel, prompt_tiles=pt, final=final),
        grid=(pt + n_s // ROW_TILE,),
        in_specs=in_specs,
        out_specs=_pair_specs(d, pt),
        out_shape=_pair_shapes(n_p, n_s, d, F32),
        compiler_params=_cparams(("arbitrary",)),
        name="ffn",
    )(*args)


def _block_diag(m):
    g, r, c = m.shape
    eye = jnp.eye(g, dtype=m.dtype)
    return (eye[:, None, :, None] * m[:, :, None, :]).reshape(g * r, g * c)


def _time_major(a, n_seq, seq):
    return a.reshape(n_seq, seq, -1).transpose(1, 0, 2).reshape(n_seq * seq, -1)


def _seq_major(a, n_seq, seq):
    return a.reshape(seq, n_seq, -1).transpose(1, 0, 2).reshape(n_seq * seq, -1)


def kernel(x_prompt, x_sample, cache_k, cache_v, state_ssm_re, state_ssm_im, page_table, norm1_g, w_in, sb_bias, ssm_a_re, ssm_a_im, ssm_log_dt, ssm_b_re, ssm_b_im, ssm_c_re, ssm_c_im, ssm_d, ssm_w_glu, w_attn_up, w_ssm_up, w_out, norm2_g, w_ffn_gate, w_ffn_up, w_ffn_down, final_norm_g):
    n_seq_p, seq_p, d = x_prompt.shape
    n_seq_s, seq_s, _ = x_sample.shape
    depth = w_in.shape[0]
    _, groups, n_state = ssm_a_re.shape
    aw = w_attn_up.shape[1]
    sw = w_ssm_up.shape[1]
    n_heads = aw // HEAD_DIM
    ns = groups * n_state
    n_p = n_seq_p * seq_p
    n_s = n_seq_s * seq_s

    x_p = x_prompt.reshape(n_p, d)
    x_s = x_sample.reshape(n_s, d)

    ab_re, ab_im, bb_re, bb_im = _discretize(
        ssm_a_re[:, :, None, :], ssm_a_im[:, :, None, :], ssm_log_dt[:, :, None, None],
        ssm_b_re.transpose(0, 1, 3, 2), ssm_b_im.transpose(0, 1, 3, 2))

    cache_kt = cache_k.transpose(0, 1, 3, 4, 2)
    cache_vt = cache_v.transpose(0, 1, 3, 4, 2)

    gpb = SSM_COL_TILE // n_state
    n_cb = groups // gpb
    blocks = lambda m: jax.vmap(_block_diag)(m.reshape(n_cb, gpb, *m.shape[1:]))

    zeros_p = jnp.zeros((n_seq_p, ns), F32)
    outs = [[] for _ in range(6)]
    kv_t = None
    for l in range(depth):
        bbd = jnp.concatenate([blocks(bb_re[l]), blocks(bb_im[l])], axis=2).astype(BF16)
        cre_bd = blocks(ssm_c_re[l].transpose(0, 2, 1)).astype(BF16)
        cim_bd = blocks(ssm_c_im[l].transpose(0, 2, 1)).astype(BF16)
        a_re_l = ab_re[l].reshape(1, ns)
        a_im_l = ab_im[l].reshape(1, ns)
        d_l = ssm_d[l].reshape(1, sw)
        wglu_l = ssm_w_glu[l].astype(BF16)

        w_in_l = w_in[l].astype(BF16)
        n_qkvu = 3 * aw + sw
        g1 = norm1_g[l].reshape(1, d)
        q_p, q_s, u_p, u_s, k_s, v_s, kt, vt = _inproj(
            x_p, x_s, g1, w_in_l[:, :n_qkvu], w_in_l[:, aw:3 * aw].T, kv_t,
            l, depth, n_seq_p, aw, sw)
        kv_t = (kt, vt)

        o_p = _attn_prompt(sb_bias[l], q_p, kt, vt, l)
        o_s = _attn_sample(page_table, sb_bias[l], q_s, k_s, v_s, cache_kt, cache_vt, l, seq_s)

        ssm_w = (a_re_l, a_im_l, bbd, cre_bd, cim_bd, d_l, wglu_l)
        y_p, hre_p, him_p = _ssm(_time_major(u_p, n_seq_p, seq_p), zeros_p, zeros_p, *ssm_w,
                                 n_tiles=n_seq_p // SSM_SEQ_TILE, steps=SSM_TIME_TILE)
        y_s, hre_s, him_s = _ssm(_time_major(u_s, n_seq_s, seq_s),
                                 state_ssm_re[l].reshape(n_seq_s, ns),
                                 state_ssm_im[l].reshape(n_seq_s, ns), *ssm_w,
                                 n_tiles=n_seq_s // SSM_SEQ_TILE, steps=seq_s)
        x_p, x_s = _merge(o_p, o_s, _seq_major(y_p, n_seq_p, seq_p), _seq_major(y_s, n_seq_s, seq_s),
                          x_p, x_s, g1, w_in_l[:, n_qkvu:], w_attn_up[l].astype(BF16),
                          w_ssm_up[l].astype(BF16), w_out[l].astype(BF16))
        x_p, x_s = _ffn(x_p, x_s, norm2_g[l].reshape(1, d), w_ffn_gate[l].astype(BF16),
                        w_ffn_up[l].astype(BF16), w_ffn_down[l].astype(BF16),
                        final_norm_g.reshape(1, d) if l == depth - 1 else None)

        for dst, val in zip(outs, (k_s, v_s, hre_p, him_p, hre_s, him_s)):
            dst.append(val)

    ks, vs, hrp, hip, hrs, his = (jnp.stack(o) for o in outs)
    heads_last = lambda a: a.reshape(depth, n_seq_p, n_heads, HEAD_DIM, seq_p).transpose(0, 1, 4, 2, 3)
    return (x_p.reshape(n_seq_p, seq_p, d),
            x_s.reshape(n_seq_s, seq_s, d),
            heads_last(kv_t[0]),
            heads_last(kv_t[1]),
            ks.reshape(depth, n_seq_s, seq_s, n_heads, HEAD_DIM),
            vs.reshape(depth, n_seq_s, seq_s, n_heads, HEAD_DIM),
            hrp.reshape(depth, n_seq_p, groups, n_state),
            hip.reshape(depth, n_seq_p, groups, n_state),
            hrs.reshape(depth, n_seq_s, groups, n_state),
            his.reshape(depth, n_seq_s, groups, n_state))
```

```python
import functools
import math

import jax
import jax.numpy as jnp
from jax import lax
from jax.experimental import pallas as pl
from jax.experimental.pallas import tpu as pltpu

F32 = jnp.float32
BF16 = jnp.bfloat16

HEAD_DIM = 64
SSM_GROUP_CH = 16
RMS_EPS = 1e-6
LOG2_E = 1.0 / math.log(2.0)
Q_SCALE = HEAD_DIM ** -0.5 * LOG2_E
ROW_TILE = 512
ATT_BLOCK = 256
ATT_HEADS = 8
ATT_SEQS = 2
SSM_SEQ_TILE = 8
SSM_TIME_TILE = 64
SSM_COL_TILE = 512
VMEM_LIMIT = 56 * 1024 * 1024


def _cparams(sem):
    return pltpu.CompilerParams(dimension_semantics=sem, vmem_limit_bytes=VMEM_LIMIT)


def _const_spec(shape):
    nd = len(shape)
    return pl.BlockSpec(shape, lambda *_: (0,) * nd)


def _layer_spec(shape, layer, col=0):
    nd = len(shape)
    return pl.BlockSpec((None,) + tuple(shape), lambda *_: (layer,) + (0,) * (nd - 1) + (col,))


def _rmsnorm(x, g):
    ms = jnp.mean(x * x, axis=-1, keepdims=True)
    return x * lax.rsqrt(ms + RMS_EPS) * g


def _bdot(a, b):
    return jnp.dot(a.astype(BF16), b.astype(BF16), preferred_element_type=F32)


def _sb_log2_terms(t):
    log_beta = jnp.minimum(t, 0.0) - jnp.log2(1.0 + jnp.exp2(-jnp.abs(t)))
    return log_beta, t - log_beta


def _split_dot(x, m):
    hi = x.astype(BF16)
    lo = (x - hi.astype(F32)).astype(BF16)
    return (jnp.dot(hi, m, preferred_element_type=F32)
            + jnp.dot(lo, m, preferred_element_type=F32))


def _pair_specs(width, prompt_tiles):
    prompt = pl.BlockSpec((ROW_TILE, width), lambda i: (jnp.minimum(i, prompt_tiles - 1), 0))
    sample = pl.BlockSpec((ROW_TILE, width), lambda i: (jnp.maximum(i - prompt_tiles, 0), 0))
    return [prompt, sample]


def _pair_shapes(n_p, n_s, width, dtype):
    return [jax.ShapeDtypeStruct((n_p, width), dtype), jax.ShapeDtypeStruct((n_s, width), dtype)]


def _read_pair(is_prompt, p_ref, s_ref):
    return jnp.where(is_prompt, p_ref[...], s_ref[...])


def _write_pair(is_prompt, p_ref, s_ref, value):
    @pl.when(is_prompt)
    def _():
        p_ref[...] = value

    @pl.when(jnp.logical_not(is_prompt))
    def _():
        s_ref[...] = value


def _inproj_kernel(xp_ref, xs_ref, g_ref, w_ref, wt_ref, *refs, aw, sw, prompt_tiles, aliased):
    (qp_ref, qs_ref, up_ref, us_ref, ks_ref, vs_ref, kt_ref, vt_ref) = refs[2 if aliased else 0:]
    is_prompt = pl.program_id(0) < prompt_tiles
    h = _rmsnorm(_read_pair(is_prompt, xp_ref, xs_ref), g_ref[...]).astype(BF16)

    def proj(c0, n):
        return jnp.dot(h, w_ref[:, c0:c0 + n], preferred_element_type=F32)

    _write_pair(is_prompt, qp_ref, qs_ref, proj(0, aw) * Q_SCALE)
    _write_pair(is_prompt, up_ref, us_ref, proj(3 * aw, sw))

    @pl.when(is_prompt)
    def _():
        nt = (((1,), (1,)), ((), ()))
        kv = lax.dot_general(wt_ref[...], h, nt, preferred_element_type=F32)
        kt_ref[...] = kv[:aw]
        vt_ref[...] = kv[aw:]

    @pl.when(jnp.logical_not(is_prompt))
    def _():
        ks_ref[...] = proj(aw, aw)
        vs_ref[...] = proj(2 * aw, aw)


def _inproj(x_p, x_s, g, w_in, w_kv_t, kv_t, layer, n_seq_p, aw, sw):
    depth = w_in.shape[0]
    n_p, d = x_p.shape
    n_s = x_s.shape[0]
    seq_p = n_p // n_seq_p
    seq_tiles = seq_p // ROW_TILE
    pt = n_p // ROW_TILE
    aliased = kv_t is not None

    def kt_map(i):
        j = jnp.minimum(i, pt - 1)
        return (layer, j // seq_tiles, 0, j % seq_tiles)

    kt_spec = pl.BlockSpec((None, None, aw, ROW_TILE), kt_map)
    stacked = jax.ShapeDtypeStruct((depth, n_seq_p, aw, seq_p), F32)
    sample_kv = _pair_specs(aw, pt)[1]
    any_spec = pl.BlockSpec(memory_space=pl.ANY)
    return pl.pallas_call(
        functools.partial(_inproj_kernel, aw=aw, sw=sw, prompt_tiles=pt, aliased=aliased),
        grid=(pt + n_s // ROW_TILE,),
        in_specs=_pair_specs(d, pt)
        + [_layer_spec((1, d), layer), _layer_spec((d, 3 * aw + sw), layer),
           _layer_spec((2 * aw, d), layer)]
        + ([any_spec, any_spec] if aliased else []),
        out_specs=_pair_specs(aw, pt) + _pair_specs(sw, pt) + [sample_kv, sample_kv, kt_spec, kt_spec],
        out_shape=_pair_shapes(n_p, n_s, aw, F32)
        + _pair_shapes(n_p, n_s, sw, F32)
        + _pair_shapes(n_p, n_s, aw, F32)[1:] * 2
        + [stacked, stacked],
        input_output_aliases={5: 6, 6: 7} if aliased else {},
        compiler_params=_cparams(("arbitrary",)),
        name="inproj",
    )(x_p, x_s, g, w_in, w_kv_t, *(kv_t if aliased else ()))


def _later_key_matrix(n):
    r = lax.broadcasted_iota(jnp.int32, (n, n), 0)
    c = lax.broadcasted_iota(jnp.int32, (n, n), 1)
    return (r > c).astype(BF16)


def _attn_prompt_kernel(bias_ref, q_ref, kt_ref, vt_ref, o_ref, acc_ref, run_ref, *, bias_base):
    grp = pl.program_id(1)
    qi = pl.program_id(2)
    blk = ATT_BLOCK
    pair = 2 * HEAD_DIM
    n_pairs = ATT_HEADS // 2
    lane = lax.broadcasted_iota(jnp.int32, (blk, pair), 1)
    in_head = (lane < HEAD_DIM, lane >= HEAD_DIM)
    feat = lax.broadcasted_iota(jnp.int32, (pair, blk), 0)
    v_rows = (feat < HEAD_DIM, feat >= HEAD_DIM)
    tri = _later_key_matrix(blk)
    r = lax.broadcasted_iota(jnp.int32, (blk, blk), 0)
    c = lax.broadcasted_iota(jnp.int32, (blk, blk), 1)
    causal = c < r
    nt = (((1,), (1,)), ((), ()))

    q = q_ref[...].astype(BF16)
    qh = [jnp.where(in_head[h % 2], q[:, (h // 2) * pair:(h // 2 + 1) * pair], 0).astype(BF16)
          for h in range(ATT_HEADS)]
    bias = [bias_ref[bias_base + grp * ATT_HEADS + h] * LOG2_E for h in range(ATT_HEADS)]
    acc_ref[...] = jnp.zeros_like(acc_ref)
    run_ref[...] = jnp.zeros_like(run_ref)

    def block(j, mask):
        start = pl.multiple_of(j * blk, blk)
        heads = range(ATT_HEADS)
        kb = [kt_ref[p * pair:(p + 1) * pair, pl.ds(start, blk)].astype(BF16)
              for p in range(n_pairs)]
        vb = [vt_ref[p * pair:(p + 1) * pair, pl.ds(start, blk)].astype(BF16)
              for p in range(n_pairs)]
        z = [jnp.dot(qh[h], kb[h // 2], preferred_element_type=F32) for h in heads]
        terms = [_sb_log2_terms(z[h] + bias[h]) for h in heads]
        log_beta = [t[0] for t in terms]
        keep = [t[1] if mask is None else jnp.where(mask, t[1], 0.0) for t in terms]
        later = [jnp.dot(keep[h].astype(BF16), tri, preferred_element_type=F32) for h in heads]
        w = [jnp.exp2(log_beta[h] - (later[h] + run_ref[h])) for h in heads]
        if mask is not None:
            w = [jnp.where(mask, w[h], 0.0) for h in heads]
        for h in heads:
            run_ref[h] = run_ref[h] + jnp.sum(keep[h], axis=-1, keepdims=True)
        for p in range(n_pairs):
            pv = [lax.dot_general(w[2 * p + hh].astype(BF16),
                                  jnp.where(v_rows[hh], vb[p], 0).astype(BF16), nt,
                                  preferred_element_type=F32) for hh in range(2)]
            acc_ref[p] = acc_ref[p] + (pv[0] + pv[1])

    block(qi, causal)

    def body(i, carry):
        block(qi - 1 - i, None)
        return carry

    lax.fori_loop(0, qi, body, 0)
    o_ref[...] = jnp.concatenate([acc_ref[p] for p in range(n_pairs)], axis=-1).astype(o_ref.dtype)


def _attn_prompt(bias, q, kt, vt, layer):
    _, n_seq, aw, seq = kt.shape
    blk = ATT_BLOCK
    nq = seq // blk
    width = ATT_HEADS * HEAD_DIM
    kv_spec = pl.BlockSpec((None, None, width, seq), lambda b, g, i: (layer, b, g, 0))
    return pl.pallas_call(
        functools.partial(_attn_prompt_kernel, bias_base=layer * (aw // HEAD_DIM)),
        grid=(n_seq, aw // width, nq),
        in_specs=[
            pl.BlockSpec(memory_space=pltpu.SMEM),
            pl.BlockSpec((blk, width), lambda b, g, i: (b * nq + i, g)),
            kv_spec, kv_spec,
        ],
        out_specs=pl.BlockSpec((blk, width), lambda b, g, i: (b * nq + i, g)),
        out_shape=jax.ShapeDtypeStruct((n_seq * seq, aw), BF16),
        scratch_shapes=[pltpu.VMEM((ATT_HEADS // 2, blk, 2 * HEAD_DIM), F32),
                        pltpu.VMEM((ATT_HEADS, blk, 1), F32)],
        compiler_params=_cparams(("parallel", "parallel", "arbitrary")),
        name="attn_prompt",
    )(bias, q, kt, vt)


def _attn_sample_kernel(pt_ref, bias_ref, q_ref, kn_ref, vn_ref, *refs, n_pages, n_heads, page, tq,
                        bias_base):
    del pt_ref
    n_page_refs = ATT_SEQS * n_pages
    k_refs = refs[:n_page_refs]
    v_refs = refs[n_page_refs:2 * n_page_refs]
    o_ref = refs[2 * n_page_refs]
    hd = HEAD_DIM
    aw = n_heads * hd
    rows = n_heads * tq
    nt = (((1,), (1,)), ((), ()))
    seqs = range(ATT_SEQS)

    row_head = lax.broadcasted_iota(jnp.int32, (rows, aw), 0) // tq
    col_head = lax.broadcasted_iota(jnp.int32, (rows, aw), 1) // hd
    on_diag = row_head == col_head
    q_bd = [jnp.where(on_diag, jnp.concatenate([q_ref[s * tq:(s + 1) * tq, :]] * n_heads, axis=0),
                      0.0).astype(BF16) for s in seqs]
    bias_head = lax.broadcasted_iota(jnp.int32, (rows, 1), 0) // tq
    bias = jnp.zeros((rows, 1), F32)
    for h in range(n_heads):
        bias = jnp.where(bias_head == h, bias_ref[bias_base + h] * LOG2_E, bias)

    pad = jnp.zeros((page - tq, aw), F32)

    def new(ref, s):
        return jnp.concatenate([ref[s * tq:(s + 1) * tq, :], pad], axis=0).astype(BF16)

    def past(ref):
        return ref[...].reshape(aw, page).astype(BF16)

    tri = _later_key_matrix(page)

    n_blocks = n_pages + 1
    z = jnp.concatenate(
        [blk for s in seqs for blk in
         [lax.dot_general(q_bd[s], new(kn_ref, s), nt, preferred_element_type=F32) + bias]
         + [jnp.dot(q_bd[s], past(k_refs[s * n_pages + j]), preferred_element_type=F32) + bias
            for j in reversed(range(n_pages))]], axis=0)
    r = lax.broadcasted_iota(jnp.int32, (ATT_SEQS * n_blocks * rows, page), 0)
    c = lax.broadcasted_iota(jnp.int32, (ATT_SEQS * n_blocks * rows, page), 1)
    mask = ((r % (n_blocks * rows)) >= rows) | (c < (r % tq))
    log_beta, neg_log_keep = _sb_log2_terms(z)
    neg_log_keep = jnp.where(mask, neg_log_keep, 0.0)
    later = _split_dot(neg_log_keep, tri)
    tot = jnp.sum(neg_log_keep, axis=-1, keepdims=True)
    runs = []
    for s in seqs:
        run = jnp.zeros((rows, 1), F32)
        for b in range(n_blocks):
            runs.append(run)
            r0 = (s * n_blocks + b) * rows
            run = run + tot[r0:r0 + rows]
    w = jnp.where(mask, jnp.exp2(log_beta - later - jnp.concatenate(runs, axis=0)), 0.0)
    w = w.astype(BF16)

    outs = []
    for s in seqs:
        r0 = s * n_blocks * rows
        acc = jnp.dot(w[r0:r0 + rows], new(vn_ref, s), preferred_element_type=F32)
        for b in range(1, n_blocks):
            acc = acc + lax.dot_general(w[r0 + b * rows:r0 + (b + 1) * rows],
                                        past(v_refs[s * n_pages + n_pages - b]), nt,
                                        preferred_element_type=F32)
        outs.append(jnp.concatenate(
            [acc[h * tq:(h + 1) * tq, h * hd:(h + 1) * hd] for h in range(n_heads)], axis=-1))
    o_ref[...] = jnp.concatenate(outs, axis=0).astype(o_ref.dtype)


def _attn_sample(page_table, bias, q, k, v, cache_k, cache_v, layer, tq):
    n_dec, n_pages = page_table.shape
    _, _, n_heads, hd, page = cache_k.shape
    aw = n_heads * hd

    def page_spec(s, j):
        return pl.BlockSpec((None, None, n_heads, hd, page),
                            lambda b, pt: (layer, pt[(b * ATT_SEQS + s) * n_pages + j], 0, 0, 0))

    pages = [page_spec(s, j) for s in range(ATT_SEQS) for j in range(n_pages)]
    row = pl.BlockSpec((ATT_SEQS * tq, aw), lambda b, pt: (b, 0))
    grid_spec = pltpu.PrefetchScalarGridSpec(
        num_scalar_prefetch=1,
        grid=(n_dec // ATT_SEQS,),
        in_specs=[pl.BlockSpec(memory_space=pltpu.SMEM), row, row, row] + pages * 2,
        out_specs=row,
    )
    return pl.pallas_call(
        functools.partial(_attn_sample_kernel, n_pages=n_pages, n_heads=n_heads, page=page, tq=tq,
                          bias_base=layer * n_heads),
        grid_spec=grid_spec,
        out_shape=jax.ShapeDtypeStruct((n_dec * tq, aw), BF16),
        compiler_params=_cparams(("parallel",)),
        name="attn_sample",
    )(page_table.reshape(-1), bias, q, k, v, *([cache_k] * len(pages)), *([cache_v] * len(pages)))


def _discretize_kernel(are_ref, aim_ref, ldt_ref, bre_ref, bim_ref,
                       abre_ref, abim_ref, bbre_ref, bbim_ref):
    a_re = are_ref[...]
    a_im = aim_ref[...]
    dt = jnp.exp(ldt_ref[...])
    mag = jnp.exp(a_re * dt)
    ang = a_im * dt
    ab_re = mag * jnp.cos(ang)
    ab_im = mag * jnp.sin(ang)
    nr = ab_re - 1.0
    den = a_re * a_re + a_im * a_im
    c_re = (nr * a_re + ab_im * a_im) / den
    c_im = (ab_im * a_re - nr * a_im) / den
    b_re = bre_ref[...]
    b_im = bim_ref[...]
    abre_ref[...] = ab_re
    abim_ref[...] = ab_im
    bbre_ref[...] = c_re * b_re - c_im * b_im
    bbim_ref[...] = c_re * b_im + c_im * b_re


def _discretize(a_re, a_im, log_dt, b_re, b_im):
    depth, g, _, n = a_re.shape
    c = b_re.shape[2]
    sa = pl.BlockSpec((None, g, 1, n), lambda l: (l, 0, 0, 0))
    sb = pl.BlockSpec((None, g, c, n), lambda l: (l, 0, 0, 0))
    return pl.pallas_call(
        _discretize_kernel,
        grid=(depth,),
        in_specs=[sa, sa, pl.BlockSpec((None, g, 1, 1), lambda l: (l, 0, 0, 0)), sb, sb],
        out_specs=(sa, sa, sb, sb),
        out_shape=(jax.ShapeDtypeStruct((depth, g, 1, n), F32),) * 2
        + (jax.ShapeDtypeStruct((depth, g, c, n), F32),) * 2,
        compiler_params=_cparams(("parallel",)),
        name="ssm_discretize",
    )(a_re, a_im, log_dt, b_re, b_im)


def _ssm_kernel(u_ref, h0re_ref, h0im_ref, are_ref, aim_ref, bbd_ref, cre_ref, cim_ref,
                d_ref, wglu_ref, y_ref, hre_ref, him_ref, bu_ref, cre_s, cim_s, *, n_tiles, steps):
    tc = pl.program_id(1)
    ns = are_ref.shape[1]
    st = SSM_SEQ_TILE

    @pl.when(tc == 0)
    def _():
        cre_s[...] = h0re_ref[...]
        cim_s[...] = h0im_ref[...]

    u = u_ref[...]
    ub = u.astype(BF16)
    n_blocks, cin, _ = bbd_ref.shape
    cs = SSM_COL_TILE
    for cg in range(n_blocks):
        bu_ref[:, 2 * cs * cg:2 * cs * (cg + 1)] = jnp.dot(
            ub[:, cin * cg:cin * (cg + 1)], bbd_ref[cg], preferred_element_type=F32)

    for cg in range(n_blocks):
        nat = slice(cg * cs, (cg + 1) * cs)
        re = slice(2 * cs * cg, 2 * cs * cg + cs)
        im = slice(2 * cs * cg + cs, 2 * cs * (cg + 1))
        ar = jnp.broadcast_to(are_ref[:, nat], (st, cs))
        ai = jnp.broadcast_to(aim_ref[:, nat], (st, cs))

        def tile(bt, _):
            s0 = pl.multiple_of(bt * st, st)

            def step(t, carry):
                hr, hi = carry
                r0 = pl.multiple_of((t * n_tiles + bt) * st, st)
                nhr = ar * hr - ai * hi + bu_ref[pl.ds(r0, st), re]
                nhi = ar * hi + ai * hr + bu_ref[pl.ds(r0, st), im]
                bu_ref[pl.ds(r0, st), re] = nhr
                bu_ref[pl.ds(r0, st), im] = nhi
                return nhr, nhi

            hr, hi = lax.fori_loop(0, steps, step,
                                   (cre_s[pl.ds(s0, st), nat], cim_s[pl.ds(s0, st), nat]),
                                   unroll=8)
            cre_s[pl.ds(s0, st), nat] = hr
            cim_s[pl.ds(s0, st), nat] = hi
            return 0

        lax.fori_loop(0, n_tiles, tile, 0)

    y = jnp.concatenate(
        [_bdot(bu_ref[:, 2 * cs * cg:2 * cs * cg + cs], cre_ref[cg])
         - _bdot(bu_ref[:, 2 * cs * cg + cs:2 * cs * (cg + 1)], cim_ref[cg])
         for cg in range(n_blocks)], axis=-1) + d_ref[...] * u
    yg = jax.nn.gelu(y)
    y_ref[...] = (yg * jax.nn.sigmoid(_bdot(yg, wglu_ref[...]))).astype(y_ref.dtype)

    @pl.when(tc == pl.num_programs(1) - 1)
    def _():
        hre_ref[...] = cre_s[...]
        him_ref[...] = cim_s[...]


def _ssm(u_tb, h0_re, h0_im, ab_re, ab_im, bbd, cre_bd, cim_bd, d_skip, w_glu, layer, n_tiles, steps):
    n, sw = u_tb.shape
    n_seq, ns = h0_re.shape
    seqs = n_tiles * SSM_SEQ_TILE
    rows = steps * seqs
    seq_blocks = n_seq // seqs
    time_blocks = n // (rows * seq_blocks)
    assert seq_blocks == 1 or time_blocks == 1
    row_map = (lambda s, t: (t, 0)) if seq_blocks == 1 else (lambda s, t: (s, 0))
    state = pl.BlockSpec((seqs, ns), lambda s, t: (s, 0))
    return pl.pallas_call(
        functools.partial(_ssm_kernel, n_tiles=n_tiles, steps=steps),
        grid=(seq_blocks, time_blocks),
        in_specs=[
            pl.BlockSpec((rows, sw), row_map), state, state,
            _layer_spec((1, ns), layer), _layer_spec((1, ns), layer),
            _layer_spec(bbd.shape[1:], layer), _layer_spec(cre_bd.shape[1:], layer),
            _layer_spec(cim_bd.shape[1:], layer),
            _layer_spec((1, sw), layer), _layer_spec((sw, sw), layer),
        ],
        out_specs=(pl.BlockSpec((rows, sw), row_map), state, state),
        out_shape=(jax.ShapeDtypeStruct((n, sw), BF16),
                   jax.ShapeDtypeStruct((n_seq, ns), F32),
                   jax.ShapeDtypeStruct((n_seq, ns), F32)),
        scratch_shapes=[pltpu.VMEM((rows, 2 * ns), F32),
                        pltpu.VMEM((seqs, ns), F32), pltpu.VMEM((seqs, ns), F32)],
        compiler_params=_cparams(("parallel", "arbitrary")),
        name="ssm",
    )(u_tb, h0_re, h0_im, ab_re, ab_im, bbd, cre_bd, cim_bd, d_skip, w_glu)


def _merge_kernel(op_ref, os_ref, yp_ref, ys_ref, xp_ref, xs_ref, g_ref, wg_ref, wa_ref, ws_ref,
                  wo_ref, outp_ref, outs_ref, *, prompt_tiles):
    is_prompt = pl.program_id(0) < prompt_tiles
    x = _read_pair(is_prompt, xp_ref, xs_ref)
    d = x.shape[1]
    h = _rmsnorm(x, g_ref[...]).astype(BF16)
    g_att = jnp.dot(h, wg_ref[:, :d], preferred_element_type=F32)
    g_ssm = jnp.dot(h, wg_ref[:, d:], preferred_element_type=F32)
    att = jnp.dot(_read_pair(is_prompt, op_ref, os_ref), wa_ref[...], preferred_element_type=F32)
    ssm = jnp.dot(_read_pair(is_prompt, yp_ref, ys_ref), ws_ref[...], preferred_element_type=F32)
    merged = jax.nn.sigmoid(g_att) * att + jax.nn.sigmoid(g_ssm) * ssm
    _write_pair(is_prompt, outp_ref, outs_ref, x + _bdot(merged, wo_ref[...]))


def _merge(o_p, o_s, y_p, y_s, x_p, x_s, g, w_in, w_attn_up, w_ssm_up, w_out, layer):
    n_p, d = x_p.shape
    n_s = x_s.shape[0]
    aw = o_p.shape[1]
    sw = y_p.shape[1]
    pt = n_p // ROW_TILE
    return pl.pallas_call(
        functools.partial(_merge_kernel, prompt_tiles=pt),
        grid=(pt + n_s // ROW_TILE,),
        in_specs=_pair_specs(aw, pt) + _pair_specs(sw, pt) + _pair_specs(d, pt)
        + [_layer_spec((1, d), layer), _layer_spec((d, 2 * d), layer, col=w_in.shape[2] // (2 * d) - 1),
           _layer_spec((aw, d), layer), _layer_spec((sw, d), layer), _layer_spec((d, d), layer)],
        out_specs=_pair_specs(d, pt),
        out_shape=_pair_shapes(n_p, n_s, d, F32),
        compiler_params=_cparams(("arbitrary",)),
        name="merge",
    )(o_p, o_s, y_p, y_s, x_p, x_s, g, w_in, w_attn_up, w_ssm_up, w_out)


def _ffn_chunks(d_ff):
    chunk = 768
    edges = list(range(0, d_ff, chunk)) + [d_ff]
    return list(zip(edges[:-1], edges[1:]))


def _ffn_kernel(xp_ref, xs_ref, g_ref, wg_ref, wu_ref, wd_ref, *rest, prompt_tiles, final):
    outp_ref, outs_ref = rest[-2:]
    is_prompt = pl.program_id(0) < prompt_tiles
    x = _read_pair(is_prompt, xp_ref, xs_ref)
    h = _rmsnorm(x, g_ref[...]).astype(BF16)
    acc = x
    for c0, c1 in _ffn_chunks(wg_ref.shape[1]):
        gate = jnp.dot(h, wg_ref[:, c0:c1], preferred_element_type=F32)
        up = jnp.dot(h, wu_ref[:, c0:c1], preferred_element_type=F32)
        acc = acc + _bdot(jax.nn.silu(gate) * up, wd_ref[c0:c1, :])
    if final:
        acc = _rmsnorm(acc, rest[0][...])
    _write_pair(is_prompt, outp_ref, outs_ref, acc)


def _ffn(x_p, x_s, g, w_gate, w_up, w_down, layer, final_g=None):
    n_p, d = x_p.shape
    n_s = x_s.shape[0]
    d_ff = w_gate.shape[2]
    pt = n_p // ROW_TILE
    final = final_g is not None
    in_specs = _pair_specs(d, pt) + [_layer_spec((1, d), layer), _layer_spec((d, d_ff), layer),
                                     _layer_spec((d, d_ff), layer), _layer_spec((d_ff, d), layer)]
    args = [x_p, x_s, g, w_gate, w_up, w_down]
    if final:
        in_specs.append(_const_spec((1, d)))
        args.append(final_g)
    return pl.pallas_call(
        functools.partial(_ffn_kernel, prompt_tiles=pt, final=final),
        grid=(pt + n_s // ROW_TILE,),
        in_specs=in_specs,
        out_specs=_pair_specs(d, pt),
        out_shape=_pair_shapes(n_p, n_s, d, F32),
        compiler_params=_cparams(("arbitrary",)),
        name="ffn",
    )(*args)


def _block_diag(m):
    g, r, c = m.shape
    eye = jnp.eye(g, dtype=m.dtype)
    return (eye[:, None, :, None] * m[:, :, None, :]).reshape(g * r, g * c)


def _time_major(a, n_seq, seq):
    return a.reshape(n_seq, seq, -1).transpose(1, 0, 2).reshape(n_seq * seq, -1)


def _seq_major(a, n_seq, seq):
    return a.reshape(seq, n_seq, -1).transpose(1, 0, 2).reshape(n_seq * seq, -1)


def kernel(x_prompt, x_sample, cache_k, cache_v, state_ssm_re, state_ssm_im, page_table, norm1_g, w_in, sb_bias, ssm_a_re, ssm_a_im, ssm_log_dt, ssm_b_re, ssm_b_im, ssm_c_re, ssm_c_im, ssm_d, ssm_w_glu, w_attn_up, w_ssm_up, w_out, norm2_g, w_ffn_gate, w_ffn_up, w_ffn_down, final_norm_g):
    n_seq_p, seq_p, d = x_prompt.shape
    n_seq_s, seq_s, _ = x_sample.shape
    depth = w_in.shape[0]
    _, groups, n_state = ssm_a_re.shape
    aw = w_attn_up.shape[1]
    sw = w_ssm_up.shape[1]
    n_heads = aw // HEAD_DIM
    ns = groups * n_state
    n_p = n_seq_p * seq_p
    n_s = n_seq_s * seq_s

    x_p = x_prompt.reshape(n_p, d)
    x_s = x_sample.reshape(n_s, d)

    ab_re, ab_im, bb_re, bb_im = _discretize(
        ssm_a_re[:, :, None, :], ssm_a_im[:, :, None, :], ssm_log_dt[:, :, None, None],
        ssm_b_re.transpose(0, 1, 3, 2), ssm_b_im.transpose(0, 1, 3, 2))

    cache_kt = cache_k.transpose(0, 1, 3, 4, 2)
    cache_vt = cache_v.transpose(0, 1, 3, 4, 2)

    gpb = SSM_COL_TILE // n_state
    n_cb = groups // gpb
    blocks = jax.vmap(lambda m: jax.vmap(_block_diag)(m.reshape(n_cb, gpb, *m.shape[1:])))

    assert 3 * aw + sw == 2 * d
    w_in_b = w_in.astype(BF16)
    w_kv_t = jnp.swapaxes(w_in_b[:, :, aw:3 * aw], 1, 2)
    g1 = norm1_g.reshape(depth, 1, d)
    g2 = norm2_g.reshape(depth, 1, d)
    w_attn_up_b, w_ssm_up_b, w_out_b = (w.astype(BF16) for w in (w_attn_up, w_ssm_up, w_out))
    w_gate_b, w_up_b, w_down_b = (w.astype(BF16) for w in (w_ffn_gate, w_ffn_up, w_ffn_down))
    ssm_w = (ab_re.reshape(depth, 1, ns), ab_im.reshape(depth, 1, ns),
             jnp.concatenate([blocks(bb_re), blocks(bb_im)], axis=3).astype(BF16),
             blocks(ssm_c_re.transpose(0, 1, 3, 2)).astype(BF16),
             blocks(ssm_c_im.transpose(0, 1, 3, 2)).astype(BF16),
             ssm_d.reshape(depth, 1, sw), ssm_w_glu.astype(BF16))
    h0_re = state_ssm_re.reshape(depth, n_seq_s, ns)
    h0_im = state_ssm_im.reshape(depth, n_seq_s, ns)
    bias = sb_bias.reshape(-1)

    zeros_p = jnp.zeros((n_seq_p, ns), F32)
    outs = [[] for _ in range(6)]
    kv_t = None
    for l in range(depth):
        q_p, q_s, u_p, u_s, k_s, v_s, kt, vt = _inproj(
            x_p, x_s, g1, w_in_b, w_kv_t, kv_t, l, n_seq_p, aw, sw)
        kv_t = (kt, vt)

        o_p = _attn_prompt(bias, q_p, kt, vt, l)
        o_s = _attn_sample(page_table, bias, q_s, k_s, v_s, cache_kt, cache_vt, l, seq_s)

        y_p, hre_p, him_p = _ssm(_time_major(u_p, n_seq_p, seq_p), zeros_p, zeros_p, *ssm_w, l,
                                 n_tiles=n_seq_p // SSM_SEQ_TILE, steps=SSM_TIME_TILE)
        y_s, hre_s, him_s = _ssm(_time_major(u_s, n_seq_s, seq_s), h0_re[l], h0_im[l], *ssm_w, l,
                                 n_tiles=n_seq_s // SSM_SEQ_TILE, steps=seq_s)
        x_p, x_s = _merge(o_p, o_s, _seq_major(y_p, n_seq_p, seq_p), _seq_major(y_s, n_seq_s, seq_s),
                          x_p, x_s, g1, w_in_b, w_attn_up_b, w_ssm_up_b, w_out_b, l)
        x_p, x_s = _ffn(x_p, x_s, g2, w_gate_b, w_up_b, w_down_b, l,
                        final_norm_g.reshape(1, d) if l == depth - 1 else None)

        for dst, val in zip(outs, (k_s, v_s, hre_p, him_p, hre_s, him_s)):
            dst.append(val)

    ks, vs, hrp, hip, hrs, his = (jnp.stack(o) for o in outs)
    heads_last = lambda a: a.reshape(depth, n_seq_p, n_heads, HEAD_DIM, seq_p).transpose(0, 1, 4, 2, 3)
    return (x_p.reshape(n_seq_p, seq_p, d),
            x_s.reshape(n_seq_s, seq_s, d),
            heads_last(kv_t[0]),
            heads_last(kv_t[1]),
            ks.reshape(depth, n_seq_s, seq_s, n_heads, HEAD_DIM),
            vs.reshape(depth, n_seq_s, seq_s, n_heads, HEAD_DIM),
            hrp.reshape(depth, n_seq_p, groups, n_state),
            hip.reshape(depth, n_seq_p, groups, n_state),
            hrs.reshape(depth, n_seq_s, groups, n_state),
            his.reshape(depth, n_seq_s, groups, n_state))
```

```python
import functools
import math

import jax
import jax.numpy as jnp
from jax import lax
from jax.experimental import pallas as pl
from jax.experimental.pallas import tpu as pltpu

F32 = jnp.float32
BF16 = jnp.bfloat16

HEAD_DIM = 64
RMS_EPS = 1e-6
LOG2_E = 1.0 / math.log(2.0)
Q_SCALE = HEAD_DIM ** -0.5 * LOG2_E
ROW_TILE = 512
ATT_BLOCK = 256
ATT_HEADS = 8
ATT_SEQS = 2
SSM_SEQ_TILE = 8
SSM_TIME_TILE = 64
SSM_COL_TILE = 512
VMEM_LIMIT = 56 * 1024 * 1024


def _cparams(sem):
    return pltpu.CompilerParams(dimension_semantics=sem, vmem_limit_bytes=VMEM_LIMIT)


def _const_spec(shape):
    nd = len(shape)
    return pl.BlockSpec(shape, lambda *_: (0,) * nd)


def _layer_spec(shape, layer, col=0):
    nd = len(shape)
    return pl.BlockSpec((None,) + tuple(shape), lambda *_: (layer,) + (0,) * (nd - 1) + (col,))


def _rmsnorm(x, g):
    ms = jnp.mean(x * x, axis=-1, keepdims=True)
    return x * lax.rsqrt(ms + RMS_EPS) * g


def _bdot(a, b):
    return jnp.dot(a.astype(BF16), b.astype(BF16), preferred_element_type=F32)


def _sb_log2_terms(t):
    log_beta = jnp.minimum(t, 0.0) - jnp.log2(1.0 + jnp.exp2(-jnp.abs(t)))
    return log_beta, t - log_beta


def _split_dot(x, m):
    hi = x.astype(BF16)
    lo = (x - hi.astype(F32)).astype(BF16)
    return (jnp.dot(hi, m, preferred_element_type=F32)
            + jnp.dot(lo, m, preferred_element_type=F32))


def _pair_specs(width, prompt_tiles):
    prompt = pl.BlockSpec((ROW_TILE, width), lambda i: (jnp.minimum(i, prompt_tiles - 1), 0))
    sample = pl.BlockSpec((ROW_TILE, width), lambda i: (jnp.maximum(i - prompt_tiles, 0), 0))
    return [prompt, sample]


def _pair_shapes(n_p, n_s, width, dtype):
    return [jax.ShapeDtypeStruct((n_p, width), dtype), jax.ShapeDtypeStruct((n_s, width), dtype)]


def _read_pair(is_prompt, p_ref, s_ref):
    return jnp.where(is_prompt, p_ref[...], s_ref[...])


def _write_pair(is_prompt, p_ref, s_ref, value):
    @pl.when(is_prompt)
    def _():
        p_ref[...] = value

    @pl.when(jnp.logical_not(is_prompt))
    def _():
        s_ref[...] = value


def _inproj_kernel(xp_ref, xs_ref, g_ref, w_ref, wt_ref, *refs, aw, sw, prompt_tiles, aliased):
    (qp_ref, qs_ref, up_ref, us_ref, ks_ref, vs_ref, kt_ref, vt_ref) = refs[2 if aliased else 0:]
    is_prompt = pl.program_id(0) < prompt_tiles
    h = _rmsnorm(_read_pair(is_prompt, xp_ref, xs_ref), g_ref[...]).astype(BF16)

    def proj(c0, n):
        return jnp.dot(h, w_ref[:, c0:c0 + n], preferred_element_type=F32)

    _write_pair(is_prompt, qp_ref, qs_ref, proj(0, aw) * Q_SCALE)
    _write_pair(is_prompt, up_ref, us_ref, proj(3 * aw, sw))

    @pl.when(is_prompt)
    def _():
        nt = (((1,), (1,)), ((), ()))
        kv = lax.dot_general(wt_ref[...], h, nt, preferred_element_type=F32)
        kt_ref[...] = kv[:aw]
        vt_ref[...] = kv[aw:]

    @pl.when(jnp.logical_not(is_prompt))
    def _():
        ks_ref[...] = proj(aw, aw)
        vs_ref[...] = proj(2 * aw, aw)


def _inproj(x_p, x_s, g, w_in, w_kv_t, kv_t, layer, n_seq_p, aw, sw):
    depth = w_in.shape[0]
    n_p, d = x_p.shape
    n_s = x_s.shape[0]
    seq_p = n_p // n_seq_p
    seq_tiles = seq_p // ROW_TILE
    pt = n_p // ROW_TILE
    aliased = kv_t is not None

    def kt_map(i):
        j = jnp.minimum(i, pt - 1)
        return (layer, j // seq_tiles, 0, j % seq_tiles)

    kt_spec = pl.BlockSpec((None, None, aw, ROW_TILE), kt_map)
    stacked = jax.ShapeDtypeStruct((depth, n_seq_p, aw, seq_p), F32)
    sample_kv = _pair_specs(aw, pt)[1]
    any_spec = pl.BlockSpec(memory_space=pl.ANY)
    return pl.pallas_call(
        functools.partial(_inproj_kernel, aw=aw, sw=sw, prompt_tiles=pt, aliased=aliased),
        grid=(pt + n_s // ROW_TILE,),
        in_specs=_pair_specs(d, pt)
        + [_layer_spec((1, d), layer), _layer_spec((d, 3 * aw + sw), layer),
           _layer_spec((2 * aw, d), layer)]
        + ([any_spec, any_spec] if aliased else []),
        out_specs=_pair_specs(aw, pt) + _pair_specs(sw, pt) + [sample_kv, sample_kv, kt_spec, kt_spec],
        out_shape=_pair_shapes(n_p, n_s, aw, F32)
        + _pair_shapes(n_p, n_s, sw, F32)
        + _pair_shapes(n_p, n_s, aw, F32)[1:] * 2
        + [stacked, stacked],
        input_output_aliases={5: 6, 6: 7} if aliased else {},
        compiler_params=_cparams(("arbitrary",)),
        name="inproj",
    )(x_p, x_s, g, w_in, w_kv_t, *(kv_t if aliased else ()))


def _later_key_matrix(n):
    r = lax.broadcasted_iota(jnp.int32, (n, n), 0)
    c = lax.broadcasted_iota(jnp.int32, (n, n), 1)
    return (r > c).astype(BF16)


def _attn_prompt_kernel(bias_ref, q_ref, kt_ref, vt_ref, o_ref, acc_ref, run_ref, *, bias_base):
    grp = pl.program_id(1)
    qi = pl.program_id(2)
    blk = ATT_BLOCK
    pair = 2 * HEAD_DIM
    n_pairs = ATT_HEADS // 2
    lane = lax.broadcasted_iota(jnp.int32, (blk, pair), 1)
    in_head = (lane < HEAD_DIM, lane >= HEAD_DIM)
    feat = lax.broadcasted_iota(jnp.int32, (pair, blk), 0)
    v_rows = (feat < HEAD_DIM, feat >= HEAD_DIM)
    tri = _later_key_matrix(blk)
    r = lax.broadcasted_iota(jnp.int32, (blk, blk), 0)
    c = lax.broadcasted_iota(jnp.int32, (blk, blk), 1)
    causal = c < r
    nt = (((1,), (1,)), ((), ()))

    q = q_ref[...].astype(BF16)
    qh = [jnp.where(in_head[h % 2], q[:, (h // 2) * pair:(h // 2 + 1) * pair], 0).astype(BF16)
          for h in range(ATT_HEADS)]
    bias = [bias_ref[bias_base + grp * ATT_HEADS + h] * LOG2_E for h in range(ATT_HEADS)]
    acc_ref[...] = jnp.zeros_like(acc_ref)
    run_ref[...] = jnp.zeros_like(run_ref)

    def block(j, mask):
        start = pl.multiple_of(j * blk, blk)
        heads = range(ATT_HEADS)
        kb = [kt_ref[p * pair:(p + 1) * pair, pl.ds(start, blk)].astype(BF16)
              for p in range(n_pairs)]
        vb = [vt_ref[p * pair:(p + 1) * pair, pl.ds(start, blk)].astype(BF16)
              for p in range(n_pairs)]
        z = [jnp.dot(qh[h], kb[h // 2], preferred_element_type=F32) for h in heads]
        terms = [_sb_log2_terms(z[h] + bias[h]) for h in heads]
        log_beta = [t[0] for t in terms]
        keep = [t[1] if mask is None else jnp.where(mask, t[1], 0.0) for t in terms]
        later = [jnp.dot(keep[h].astype(BF16), tri, preferred_element_type=F32) for h in heads]
        w = [jnp.exp2(log_beta[h] - (later[h] + run_ref[h])) for h in heads]
        if mask is not None:
            w = [jnp.where(mask, w[h], 0.0) for h in heads]
        for h in heads:
            run_ref[h] = run_ref[h] + jnp.sum(keep[h], axis=-1, keepdims=True)
        for p in range(n_pairs):
            pv = [lax.dot_general(w[2 * p + hh].astype(BF16),
                                  jnp.where(v_rows[hh], vb[p], 0).astype(BF16), nt,
                                  preferred_element_type=F32) for hh in range(2)]
            acc_ref[p] = acc_ref[p] + (pv[0] + pv[1])

    block(qi, causal)

    def body(i, carry):
        block(qi - 1 - i, None)
        return carry

    lax.fori_loop(0, qi, body, 0)
    o_ref[...] = jnp.concatenate([acc_ref[p] for p in range(n_pairs)], axis=-1).astype(o_ref.dtype)


def _attn_prompt(bias, q, kt, vt, layer):
    _, n_seq, aw, seq = kt.shape
    blk = ATT_BLOCK
    nq = seq // blk
    width = ATT_HEADS * HEAD_DIM
    kv_spec = pl.BlockSpec((None, None, width, seq), lambda b, g, i: (layer, b, g, 0))
    return pl.pallas_call(
        functools.partial(_attn_prompt_kernel, bias_base=layer * (aw // HEAD_DIM)),
        grid=(n_seq, aw // width, nq),
        in_specs=[
            pl.BlockSpec(memory_space=pltpu.SMEM),
            pl.BlockSpec((blk, width), lambda b, g, i: (b * nq + i, g)),
            kv_spec, kv_spec,
        ],
        out_specs=pl.BlockSpec((blk, width), lambda b, g, i: (b * nq + i, g)),
        out_shape=jax.ShapeDtypeStruct((n_seq * seq, aw), BF16),
        scratch_shapes=[pltpu.VMEM((ATT_HEADS // 2, blk, 2 * HEAD_DIM), F32),
                        pltpu.VMEM((ATT_HEADS, blk, 1), F32)],
        compiler_params=_cparams(("parallel", "parallel", "arbitrary")),
        name="attn_prompt",
    )(bias, q, kt, vt)


def _attn_sample_kernel(pt_ref, bias_ref, q_ref, kn_ref, vn_ref, *refs, n_pages, n_heads, page, tq,
                        bias_base):
    del pt_ref
    n_page_refs = ATT_SEQS * n_pages
    k_refs = refs[:n_page_refs]
    v_refs = refs[n_page_refs:2 * n_page_refs]
    o_ref = refs[2 * n_page_refs]
    hd = HEAD_DIM
    aw = n_heads * hd
    rows = n_heads * tq
    nt = (((1,), (1,)), ((), ()))
    seqs = range(ATT_SEQS)

    row_head = lax.broadcasted_iota(jnp.int32, (rows, aw), 0) // tq
    col_head = lax.broadcasted_iota(jnp.int32, (rows, aw), 1) // hd
    on_diag = row_head == col_head
    q_bd = [jnp.where(on_diag, jnp.concatenate([q_ref[s * tq:(s + 1) * tq, :]] * n_heads, axis=0),
                      0.0).astype(BF16) for s in seqs]
    bias_head = lax.broadcasted_iota(jnp.int32, (rows, 1), 0) // tq
    bias = jnp.zeros((rows, 1), F32)
    for h in range(n_heads):
        bias = jnp.where(bias_head == h, bias_ref[bias_base + h] * LOG2_E, bias)

    pad = jnp.zeros((page - tq, aw), F32)

    def new(ref, s):
        return jnp.concatenate([ref[s * tq:(s + 1) * tq, :], pad], axis=0).astype(BF16)

    def past(ref):
        return ref[...].reshape(aw, page).astype(BF16)

    tri = _later_key_matrix(page)

    n_blocks = n_pages + 1
    z = jnp.concatenate(
        [blk for s in seqs for blk in
         [lax.dot_general(q_bd[s], new(kn_ref, s), nt, preferred_element_type=F32) + bias]
         + [jnp.dot(q_bd[s], past(k_refs[s * n_pages + j]), preferred_element_type=F32) + bias
            for j in reversed(range(n_pages))]], axis=0)
    r = lax.broadcasted_iota(jnp.int32, (ATT_SEQS * n_blocks * rows, page), 0)
    c = lax.broadcasted_iota(jnp.int32, (ATT_SEQS * n_blocks * rows, page), 1)
    mask = ((r % (n_blocks * rows)) >= rows) | (c < (r % tq))
    log_beta, neg_log_keep = _sb_log2_terms(z)
    neg_log_keep = jnp.where(mask, neg_log_keep, 0.0)
    later = _split_dot(neg_log_keep, tri)
    tot = jnp.sum(neg_log_keep, axis=-1, keepdims=True)
    runs = []
    for s in seqs:
        run = jnp.zeros((rows, 1), F32)
        for b in range(n_blocks):
            runs.append(run)
            r0 = (s * n_blocks + b) * rows
            run = run + tot[r0:r0 + rows]
    w = jnp.where(mask, jnp.exp2(log_beta - later - jnp.concatenate(runs, axis=0)), 0.0)
    w = w.astype(BF16)

    outs = []
    for s in seqs:
        r0 = s * n_blocks * rows
        acc = jnp.dot(w[r0:r0 + rows], new(vn_ref, s), preferred_element_type=F32)
        for b in range(1, n_blocks):
            acc = acc + lax.dot_general(w[r0 + b * rows:r0 + (b + 1) * rows],
                                        past(v_refs[s * n_pages + n_pages - b]), nt,
                                        preferred_element_type=F32)
        outs.append(jnp.concatenate(
            [acc[h * tq:(h + 1) * tq, h * hd:(h + 1) * hd] for h in range(n_heads)], axis=-1))
    o_ref[...] = jnp.concatenate(outs, axis=0).astype(o_ref.dtype)


def _attn_sample(page_table, bias, q, k, v, cache_k, cache_v, layer, tq):
    n_dec, n_pages = page_table.shape
    _, _, n_heads, hd, page = cache_k.shape
    aw = n_heads * hd

    def page_spec(s, j):
        return pl.BlockSpec((None, None, n_heads, hd, page),
                            lambda b, pt: (layer, pt[(b * ATT_SEQS + s) * n_pages + j], 0, 0, 0))

    pages = [page_spec(s, j) for s in range(ATT_SEQS) for j in range(n_pages)]
    row = pl.BlockSpec((ATT_SEQS * tq, aw), lambda b, pt: (b, 0))
    grid_spec = pltpu.PrefetchScalarGridSpec(
        num_scalar_prefetch=1,
        grid=(n_dec // ATT_SEQS,),
        in_specs=[pl.BlockSpec(memory_space=pltpu.SMEM), row, row, row] + pages * 2,
        out_specs=row,
    )
    return pl.pallas_call(
        functools.partial(_attn_sample_kernel, n_pages=n_pages, n_heads=n_heads, page=page, tq=tq,
                          bias_base=layer * n_heads),
        grid_spec=grid_spec,
        out_shape=jax.ShapeDtypeStruct((n_dec * tq, aw), BF16),
        compiler_params=_cparams(("parallel",)),
        name="attn_sample",
    )(page_table.reshape(-1), bias, q, k, v, *([cache_k] * len(pages)), *([cache_v] * len(pages)))


def _discretize_kernel(are_ref, aim_ref, ldt_ref, bre_ref, bim_ref,
                       abre_ref, abim_ref, bbre_ref, bbim_ref):
    a_re = are_ref[...]
    a_im = aim_ref[...]
    dt = jnp.exp(ldt_ref[...])
    mag = jnp.exp(a_re * dt)
    ang = a_im * dt
    ab_re = mag * jnp.cos(ang)
    ab_im = mag * jnp.sin(ang)
    nr = ab_re - 1.0
    den = a_re * a_re + a_im * a_im
    c_re = (nr * a_re + ab_im * a_im) / den
    c_im = (ab_im * a_re - nr * a_im) / den
    b_re = bre_ref[...]
    b_im = bim_ref[...]
    abre_ref[...] = ab_re
    abim_ref[...] = ab_im
    bbre_ref[...] = c_re * b_re - c_im * b_im
    bbim_ref[...] = c_re * b_im + c_im * b_re


def _discretize(a_re, a_im, log_dt, b_re, b_im):
    depth, g, _, n = a_re.shape
    c = b_re.shape[2]
    sa = pl.BlockSpec((None, g, 1, n), lambda l: (l, 0, 0, 0))
    sb = pl.BlockSpec((None, g, c, n), lambda l: (l, 0, 0, 0))
    return pl.pallas_call(
        _discretize_kernel,
        grid=(depth,),
        in_specs=[sa, sa, pl.BlockSpec((None, g, 1, 1), lambda l: (l, 0, 0, 0)), sb, sb],
        out_specs=(sa, sa, sb, sb),
        out_shape=(jax.ShapeDtypeStruct((depth, g, 1, n), F32),) * 2
        + (jax.ShapeDtypeStruct((depth, g, c, n), F32),) * 2,
        compiler_params=_cparams(("parallel",)),
        name="ssm_discretize",
    )(a_re, a_im, log_dt, b_re, b_im)


def _ssm_kernel(u_ref, h0re_ref, h0im_ref, are_ref, aim_ref, bbd_ref, cre_ref, cim_ref,
                d_ref, wglu_ref, y_ref, hre_ref, him_ref, bu_ref, cre_s, cim_s, *, n_tiles, steps):
    tc = pl.program_id(1)
    st = SSM_SEQ_TILE

    @pl.when(tc == 0)
    def _():
        cre_s[...] = h0re_ref[...]
        cim_s[...] = h0im_ref[...]

    seqs = n_tiles * st
    u = jnp.swapaxes(u_ref[...], 0, 1).reshape(steps * seqs, u_ref.shape[2])
    ub = u.astype(BF16)
    n_blocks, cin, _ = bbd_ref.shape
    cs = SSM_COL_TILE
    for cg in range(n_blocks):
        bu_ref[:, 2 * cs * cg:2 * cs * (cg + 1)] = jnp.dot(
            ub[:, cin * cg:cin * (cg + 1)], bbd_ref[cg], preferred_element_type=F32)

    for cg in range(n_blocks):
        nat = slice(cg * cs, (cg + 1) * cs)
        re = slice(2 * cs * cg, 2 * cs * cg + cs)
        im = slice(2 * cs * cg + cs, 2 * cs * (cg + 1))
        ar = jnp.broadcast_to(are_ref[:, nat], (st, cs))
        ai = jnp.broadcast_to(aim_ref[:, nat], (st, cs))

        def tile(bt, _):
            s0 = pl.multiple_of(bt * st, st)

            def step(t, carry):
                hr, hi = carry
                r0 = pl.multiple_of((t * n_tiles + bt) * st, st)
                nhr = ar * hr - ai * hi + bu_ref[pl.ds(r0, st), re]
                nhi = ar * hi + ai * hr + bu_ref[pl.ds(r0, st), im]
                bu_ref[pl.ds(r0, st), re] = nhr
                bu_ref[pl.ds(r0, st), im] = nhi
                return nhr, nhi

            hr, hi = lax.fori_loop(0, steps, step,
                                   (cre_s[pl.ds(s0, st), nat], cim_s[pl.ds(s0, st), nat]),
                                   unroll=8)
            cre_s[pl.ds(s0, st), nat] = hr
            cim_s[pl.ds(s0, st), nat] = hi
            return 0

        lax.fori_loop(0, n_tiles, tile, 0)

    y = jnp.concatenate(
        [_bdot(bu_ref[:, 2 * cs * cg:2 * cs * cg + cs], cre_ref[cg])
         - _bdot(bu_ref[:, 2 * cs * cg + cs:2 * cs * (cg + 1)], cim_ref[cg])
         for cg in range(n_blocks)], axis=-1) + d_ref[...] * u
    yg = jax.nn.gelu(y)
    out = yg * jax.nn.sigmoid(_bdot(yg, wglu_ref[...]))
    y_ref[...] = jnp.swapaxes(out.reshape(steps, seqs, out.shape[1]), 0, 1).astype(y_ref.dtype)

    @pl.when(tc == pl.num_programs(1) - 1)
    def _():
        hre_ref[...] = cre_s[...]
        him_ref[...] = cim_s[...]


def _ssm(u, h0_re, h0_im, ab_re, ab_im, bbd, cre_bd, cim_bd, d_skip, w_glu, layer, n_tiles, steps):
    n_seq, seq, sw = u.shape
    ns = h0_re.shape[1]
    seqs = n_tiles * SSM_SEQ_TILE
    rows = steps * seqs
    block = pl.BlockSpec((seqs, steps, sw), lambda s, t: (s, t, 0))
    state = pl.BlockSpec((seqs, ns), lambda s, t: (s, 0))
    return pl.pallas_call(
        functools.partial(_ssm_kernel, n_tiles=n_tiles, steps=steps),
        grid=(n_seq // seqs, seq // steps),
        in_specs=[
            block, state, state,
            _layer_spec((1, ns), layer), _layer_spec((1, ns), layer),
            _layer_spec(bbd.shape[1:], layer), _layer_spec(cre_bd.shape[1:], layer),
            _layer_spec(cim_bd.shape[1:], layer),
            _layer_spec((1, sw), layer), _layer_spec((sw, sw), layer),
        ],
        out_specs=(block, state, state),
        out_shape=(jax.ShapeDtypeStruct((n_seq, seq, sw), BF16),
                   jax.ShapeDtypeStruct((n_seq, ns), F32),
                   jax.ShapeDtypeStruct((n_seq, ns), F32)),
        scratch_shapes=[pltpu.VMEM((rows, 2 * ns), F32),
                        pltpu.VMEM((seqs, ns), F32), pltpu.VMEM((seqs, ns), F32)],
        compiler_params=_cparams(("parallel", "arbitrary")),
        name="ssm",
    )(u, h0_re, h0_im, ab_re, ab_im, bbd, cre_bd, cim_bd, d_skip, w_glu)


def _merge_kernel(op_ref, os_ref, yp_ref, ys_ref, xp_ref, xs_ref, g_ref, wg_ref, wa_ref, ws_ref,
                  wo_ref, outp_ref, outs_ref, *, prompt_tiles):
    is_prompt = pl.program_id(0) < prompt_tiles
    x = _read_pair(is_prompt, xp_ref, xs_ref)
    d = x.shape[1]
    h = _rmsnorm(x, g_ref[...]).astype(BF16)
    g_att = jnp.dot(h, wg_ref[:, :d], preferred_element_type=F32)
    g_ssm = jnp.dot(h, wg_ref[:, d:], preferred_element_type=F32)
    att = jnp.dot(_read_pair(is_prompt, op_ref, os_ref), wa_ref[...], preferred_element_type=F32)
    ssm = jnp.dot(_read_pair(is_prompt, yp_ref, ys_ref), ws_ref[...], preferred_element_type=F32)
    merged = jax.nn.sigmoid(g_att) * att + jax.nn.sigmoid(g_ssm) * ssm
    _write_pair(is_prompt, outp_ref, outs_ref, x + _bdot(merged, wo_ref[...]))


def _merge(o_p, o_s, y_p, y_s, x_p, x_s, g, w_in, w_attn_up, w_ssm_up, w_out, layer):
    n_p, d = x_p.shape
    n_s = x_s.shape[0]
    aw = o_p.shape[1]
    sw = y_p.shape[1]
    pt = n_p // ROW_TILE
    return pl.pallas_call(
        functools.partial(_merge_kernel, prompt_tiles=pt),
        grid=(pt + n_s // ROW_TILE,),
        in_specs=_pair_specs(aw, pt) + _pair_specs(sw, pt) + _pair_specs(d, pt)
        + [_layer_spec((1, d), layer), _layer_spec((d, 2 * d), layer, col=w_in.shape[2] // (2 * d) - 1),
           _layer_spec((aw, d), layer), _layer_spec((sw, d), layer), _layer_spec((d, d), layer)],
        out_specs=_pair_specs(d, pt),
        out_shape=_pair_shapes(n_p, n_s, d, F32),
        compiler_params=_cparams(("arbitrary",)),
        name="merge",
    )(o_p, o_s, y_p, y_s, x_p, x_s, g, w_in, w_attn_up, w_ssm_up, w_out)


def _ffn_chunks(d_ff):
    chunk = 768
    edges = list(range(0, d_ff, chunk)) + [d_ff]
    return list(zip(edges[:-1], edges[1:]))


def _ffn_kernel(xp_ref, xs_ref, g_ref, wg_ref, wu_ref, wd_ref, *rest, prompt_tiles, final):
    outp_ref, outs_ref = rest[-2:]
    is_prompt = pl.program_id(0) < prompt_tiles
    x = _read_pair(is_prompt, xp_ref, xs_ref)
    h = _rmsnorm(x, g_ref[...]).astype(BF16)
    acc = x
    for c0, c1 in _ffn_chunks(wg_ref.shape[1]):
        gate = jnp.dot(h, wg_ref[:, c0:c1], preferred_element_type=F32)
        up = jnp.dot(h, wu_ref[:, c0:c1], preferred_element_type=F32)
        acc = acc + _bdot(jax.nn.silu(gate) * up, wd_ref[c0:c1, :])
    if final:
        acc = _rmsnorm(acc, rest[0][...])
    _write_pair(is_prompt, outp_ref, outs_ref, acc)


def _ffn(x_p, x_s, g, w_gate, w_up, w_down, layer, final_g=None):
    n_p, d = x_p.shape
    n_s = x_s.shape[0]
    d_ff = w_gate.shape[2]
    pt = n_p // ROW_TILE
    final = final_g is not None
    in_specs = _pair_specs(d, pt) + [_layer_spec((1, d), layer), _layer_spec((d, d_ff), layer),
                                     _layer_spec((d, d_ff), layer), _layer_spec((d_ff, d), layer)]
    args = [x_p, x_s, g, w_gate, w_up, w_down]
    if final:
        in_specs.append(_const_spec((1, d)))
        args.append(final_g)
    return pl.pallas_call(
        functools.partial(_ffn_kernel, prompt_tiles=pt, final=final),
        grid=(pt + n_s // ROW_TILE,),
        in_specs=in_specs,
        out_specs=_pair_specs(d, pt),
        out_shape=_pair_shapes(n_p, n_s, d, F32),
        compiler_params=_cparams(("arbitrary",)),
        name="ffn",
    )(*args)


def _block_diag(m):
    g, r, c = m.shape
    eye = jnp.eye(g, dtype=m.dtype)
    return (eye[:, None, :, None] * m[:, :, None, :]).reshape(g * r, g * c)


def kernel(x_prompt, x_sample, cache_k, cache_v, state_ssm_re, state_ssm_im, page_table, norm1_g, w_in, sb_bias, ssm_a_re, ssm_a_im, ssm_log_dt, ssm_b_re, ssm_b_im, ssm_c_re, ssm_c_im, ssm_d, ssm_w_glu, w_attn_up, w_ssm_up, w_out, norm2_g, w_ffn_gate, w_ffn_up, w_ffn_down, final_norm_g):
    n_seq_p, seq_p, d = x_prompt.shape
    n_seq_s, seq_s, _ = x_sample.shape
    depth = w_in.shape[0]
    _, groups, n_state = ssm_a_re.shape
    aw = w_attn_up.shape[1]
    sw = w_ssm_up.shape[1]
    n_heads = aw // HEAD_DIM
    ns = groups * n_state
    n_p = n_seq_p * seq_p
    n_s = n_seq_s * seq_s

    x_p = x_prompt.reshape(n_p, d)
    x_s = x_sample.reshape(n_s, d)

    ab_re, ab_im, bb_re, bb_im = _discretize(
        ssm_a_re[:, :, None, :], ssm_a_im[:, :, None, :], ssm_log_dt[:, :, None, None],
        ssm_b_re.transpose(0, 1, 3, 2), ssm_b_im.transpose(0, 1, 3, 2))

    cache_kt = cache_k.transpose(0, 1, 3, 4, 2)
    cache_vt = cache_v.transpose(0, 1, 3, 4, 2)

    gpb = SSM_COL_TILE // n_state
    n_cb = groups // gpb
    blocks = jax.vmap(lambda m: jax.vmap(_block_diag)(m.reshape(n_cb, gpb, *m.shape[1:])))

    assert 3 * aw + sw == 2 * d
    w_in_b = w_in.astype(BF16)
    w_kv_t = jnp.swapaxes(w_in_b[:, :, aw:3 * aw], 1, 2)
    g1 = norm1_g.reshape(depth, 1, d)
    g2 = norm2_g.reshape(depth, 1, d)
    w_attn_up_b, w_ssm_up_b, w_out_b = (w.astype(BF16) for w in (w_attn_up, w_ssm_up, w_out))
    w_gate_b, w_up_b, w_down_b = (w.astype(BF16) for w in (w_ffn_gate, w_ffn_up, w_ffn_down))
    ssm_w = (ab_re.reshape(depth, 1, ns), ab_im.reshape(depth, 1, ns),
             jnp.concatenate([blocks(bb_re), blocks(bb_im)], axis=3).astype(BF16),
             blocks(ssm_c_re.transpose(0, 1, 3, 2)).astype(BF16),
             blocks(ssm_c_im.transpose(0, 1, 3, 2)).astype(BF16),
             ssm_d.reshape(depth, 1, sw), ssm_w_glu.astype(BF16))
    h0_re = state_ssm_re.reshape(depth, n_seq_s, ns)
    h0_im = state_ssm_im.reshape(depth, n_seq_s, ns)
    bias = sb_bias.reshape(-1)

    zeros_p = jnp.zeros((n_seq_p, ns), F32)
    outs = [[] for _ in range(6)]
    kv_t = None
    for l in range(depth):
        q_p, q_s, u_p, u_s, k_s, v_s, kt, vt = _inproj(
            x_p, x_s, g1, w_in_b, w_kv_t, kv_t, l, n_seq_p, aw, sw)
        kv_t = (kt, vt)

        o_p = _attn_prompt(bias, q_p, kt, vt, l)
        o_s = _attn_sample(page_table, bias, q_s, k_s, v_s, cache_kt, cache_vt, l, seq_s)

        y_p, hre_p, him_p = _ssm(u_p.reshape(n_seq_p, seq_p, sw), zeros_p, zeros_p, *ssm_w, l,
                                 n_tiles=n_seq_p // SSM_SEQ_TILE, steps=SSM_TIME_TILE)
        y_s, hre_s, him_s = _ssm(u_s.reshape(n_seq_s, seq_s, sw), h0_re[l], h0_im[l], *ssm_w, l,
                                 n_tiles=n_seq_s // SSM_SEQ_TILE, steps=seq_s)
        x_p, x_s = _merge(o_p, o_s, y_p.reshape(n_p, sw), y_s.reshape(n_s, sw),
                          x_p, x_s, g1, w_in_b, w_attn_up_b, w_ssm_up_b, w_out_b, l)
        x_p, x_s = _ffn(x_p, x_s, g2, w_gate_b, w_up_b, w_down_b, l,
                        final_norm_g.reshape(1, d) if l == depth - 1 else None)

        for dst, val in zip(outs, (k_s, v_s, hre_p, him_p, hre_s, him_s)):
            dst.append(val)

    ks, vs, hrp, hip, hrs, his = (jnp.stack(o) for o in outs)
    heads_last = lambda a: a.reshape(depth, n_seq_p, n_heads, HEAD_DIM, seq_p).transpose(0, 1, 4, 2, 3)
    return (x_p.reshape(n_seq_p, seq_p, d),
            x_s.reshape(n_seq_s, seq_s, d),
            heads_last(kv_t[0]),
            heads_last(kv_t[1]),
            ks.reshape(depth, n_seq_s, seq_s, n_heads, HEAD_DIM),
            vs.reshape(depth, n_seq_s, seq_s, n_heads, HEAD_DIM),
            hrp.reshape(depth, n_seq_p, groups, n_state),
            hip.reshape(depth, n_seq_p, groups, n_state),
            hrs.reshape(depth, n_seq_s, groups, n_state),
            his.reshape(depth, n_seq_s, groups, n_state))
```

```python
import functools
import math

import jax
import jax.numpy as jnp
from jax import lax
from jax.experimental import pallas as pl
from jax.experimental.pallas import tpu as pltpu

F32 = jnp.float32
BF16 = jnp.bfloat16

HEAD_DIM = 64
RMS_EPS = 1e-6
LOG2_E = 1.0 / math.log(2.0)
Q_SCALE = HEAD_DIM ** -0.5 * LOG2_E
ROW_TILE = 512
ATT_BLOCK = 256
ATT_HEADS = 8
ATT_SEQS = 2
SSM_SEQ_TILE = 8
SSM_TIME_TILE = 64
SSM_COL_TILE = 512
VMEM_LIMIT = 56 * 1024 * 1024


def _cparams(sem):
    return pltpu.CompilerParams(dimension_semantics=sem, vmem_limit_bytes=VMEM_LIMIT)


def _const_spec(shape):
    nd = len(shape)
    return pl.BlockSpec(shape, lambda *_: (0,) * nd)


def _layer_spec(shape, layer, col=0):
    nd = len(shape)
    return pl.BlockSpec((None,) + tuple(shape), lambda *_: (layer,) + (0,) * (nd - 1) + (col,))


def _rmsnorm(x, g):
    ms = jnp.mean(x * x, axis=-1, keepdims=True)
    return x * lax.rsqrt(ms + RMS_EPS) * g


def _bdot(a, b):
    return jnp.dot(a.astype(BF16), b.astype(BF16), preferred_element_type=F32)


def _sb_log2_terms(t):
    log_beta = jnp.minimum(t, 0.0) - jnp.log2(1.0 + jnp.exp2(-jnp.abs(t)))
    return log_beta, t - log_beta


def _split_dot(x, m):
    hi = x.astype(BF16)
    lo = (x - hi.astype(F32)).astype(BF16)
    return (jnp.dot(hi, m, preferred_element_type=F32)
            + jnp.dot(lo, m, preferred_element_type=F32))


def _pair_specs(width, prompt_tiles):
    prompt = pl.BlockSpec((ROW_TILE, width), lambda i: (jnp.minimum(i, prompt_tiles - 1), 0))
    sample = pl.BlockSpec((ROW_TILE, width), lambda i: (jnp.maximum(i - prompt_tiles, 0), 0))
    return [prompt, sample]


def _pair_shapes(n_p, n_s, width, dtype):
    return [jax.ShapeDtypeStruct((n_p, width), dtype), jax.ShapeDtypeStruct((n_s, width), dtype)]


def _read_pair(is_prompt, p_ref, s_ref):
    return jnp.where(is_prompt, p_ref[...], s_ref[...])


def _write_pair(is_prompt, p_ref, s_ref, value):
    @pl.when(is_prompt)
    def _():
        p_ref[...] = value

    @pl.when(jnp.logical_not(is_prompt))
    def _():
        s_ref[...] = value


def _inproj_kernel(xp_ref, xs_ref, g_ref, w_ref, wt_ref, *refs, aw, sw, prompt_tiles, aliased):
    (qp_ref, qs_ref, up_ref, us_ref, ks_ref, vs_ref, kt_ref, vt_ref) = refs[2 if aliased else 0:]
    is_prompt = pl.program_id(0) < prompt_tiles
    h = _rmsnorm(_read_pair(is_prompt, xp_ref, xs_ref), g_ref[...]).astype(BF16)

    def proj(c0, n):
        return jnp.dot(h, w_ref[:, c0:c0 + n], preferred_element_type=F32)

    _write_pair(is_prompt, qp_ref, qs_ref, proj(0, aw) * Q_SCALE)
    _write_pair(is_prompt, up_ref, us_ref, proj(3 * aw, sw))

    @pl.when(is_prompt)
    def _():
        nt = (((1,), (1,)), ((), ()))
        kv = lax.dot_general(wt_ref[...], h, nt, preferred_element_type=F32)
        kt_ref[...] = kv[:aw]
        vt_ref[...] = kv[aw:]

    @pl.when(jnp.logical_not(is_prompt))
    def _():
        ks_ref[...] = proj(aw, aw)
        vs_ref[...] = proj(2 * aw, aw)


def _inproj(x_p, x_s, g, w_in, w_kv_t, kv_t, layer, n_seq_p, aw, sw):
    depth = w_in.shape[0]
    n_p, d = x_p.shape
    n_s = x_s.shape[0]
    seq_p = n_p // n_seq_p
    seq_tiles = seq_p // ROW_TILE
    pt = n_p // ROW_TILE
    aliased = kv_t is not None

    def kt_map(i):
        j = jnp.minimum(i, pt - 1)
        return (layer, j // seq_tiles, 0, j % seq_tiles)

    kt_spec = pl.BlockSpec((None, None, aw, ROW_TILE), kt_map)
    stacked = jax.ShapeDtypeStruct((depth, n_seq_p, aw, seq_p), F32)
    sample_kv = _pair_specs(aw, pt)[1]
    any_spec = pl.BlockSpec(memory_space=pl.ANY)
    return pl.pallas_call(
        functools.partial(_inproj_kernel, aw=aw, sw=sw, prompt_tiles=pt, aliased=aliased),
        grid=(pt + n_s // ROW_TILE,),
        in_specs=_pair_specs(d, pt)
        + [_layer_spec((1, d), layer), _layer_spec((d, 3 * aw + sw), layer),
           _layer_spec((2 * aw, d), layer)]
        + ([any_spec, any_spec] if aliased else []),
        out_specs=_pair_specs(aw, pt) + _pair_specs(sw, pt) + [sample_kv, sample_kv, kt_spec, kt_spec],
        out_shape=_pair_shapes(n_p, n_s, aw, F32)
        + _pair_shapes(n_p, n_s, sw, F32)
        + _pair_shapes(n_p, n_s, aw, F32)[1:] * 2
        + [stacked, stacked],
        input_output_aliases={5: 6, 6: 7} if aliased else {},
        compiler_params=_cparams(("arbitrary",)),
        name="inproj",
    )(x_p, x_s, g, w_in, w_kv_t, *(kv_t if aliased else ()))


def _later_key_matrix(n):
    r = lax.broadcasted_iota(jnp.int32, (n, n), 0)
    c = lax.broadcasted_iota(jnp.int32, (n, n), 1)
    return (r > c).astype(BF16)


def _attn_prompt_kernel(bias_ref, q_ref, kt_ref, vt_ref, o_ref, acc_ref, run_ref, *, bias_base):
    grp = pl.program_id(1)
    qi = pl.program_id(2)
    blk = ATT_BLOCK
    pair = 2 * HEAD_DIM
    n_pairs = ATT_HEADS // 2
    lane = lax.broadcasted_iota(jnp.int32, (blk, pair), 1)
    in_head = (lane < HEAD_DIM, lane >= HEAD_DIM)
    feat = lax.broadcasted_iota(jnp.int32, (pair, blk), 0)
    v_rows = (feat < HEAD_DIM, feat >= HEAD_DIM)
    tri = _later_key_matrix(blk)
    r = lax.broadcasted_iota(jnp.int32, (blk, blk), 0)
    c = lax.broadcasted_iota(jnp.int32, (blk, blk), 1)
    causal = c < r
    nt = (((1,), (1,)), ((), ()))

    q = q_ref[...].astype(BF16)
    qh = [jnp.where(in_head[h % 2], q[:, (h // 2) * pair:(h // 2 + 1) * pair], 0).astype(BF16)
          for h in range(ATT_HEADS)]
    bias = [bias_ref[bias_base + grp * ATT_HEADS + h] * LOG2_E for h in range(ATT_HEADS)]
    acc_ref[...] = jnp.zeros_like(acc_ref)
    run_ref[...] = jnp.zeros_like(run_ref)

    def block(j, mask):
        start = pl.multiple_of(j * blk, blk)
        heads = range(ATT_HEADS)
        kb = [kt_ref[p * pair:(p + 1) * pair, pl.ds(start, blk)].astype(BF16)
              for p in range(n_pairs)]
        vb = [vt_ref[p * pair:(p + 1) * pair, pl.ds(start, blk)].astype(BF16)
              for p in range(n_pairs)]
        z = [jnp.dot(qh[h], kb[h // 2], preferred_element_type=F32) for h in heads]
        terms = [_sb_log2_terms(z[h] + bias[h]) for h in heads]
        log_beta = [t[0] for t in terms]
        keep = [t[1] if mask is None else jnp.where(mask, t[1], 0.0) for t in terms]
        later = [jnp.dot(keep[h].astype(BF16), tri, preferred_element_type=F32) for h in heads]
        w = [jnp.exp2(log_beta[h] - (later[h] + run_ref[h])) for h in heads]
        if mask is not None:
            w = [jnp.where(mask, w[h], 0.0) for h in heads]
        for h in heads:
            run_ref[h] = run_ref[h] + jnp.sum(keep[h], axis=-1, keepdims=True)
        for p in range(n_pairs):
            pv = [lax.dot_general(w[2 * p + hh].astype(BF16),
                                  jnp.where(v_rows[hh], vb[p], 0).astype(BF16), nt,
                                  preferred_element_type=F32) for hh in range(2)]
            acc_ref[p] = acc_ref[p] + (pv[0] + pv[1])

    block(qi, causal)

    def body(i, carry):
        block(qi - 1 - 2 * i, None)
        block(qi - 2 - 2 * i, None)
        return carry

    lax.fori_loop(0, qi // 2, body, 0)

    @pl.when(qi % 2 == 1)
    def _():
        block(0, None)
    o_ref[...] = jnp.concatenate([acc_ref[p] for p in range(n_pairs)], axis=-1).astype(o_ref.dtype)


def _attn_prompt(bias, q, kt, vt, layer):
    _, n_seq, aw, seq = kt.shape
    blk = ATT_BLOCK
    nq = seq // blk
    width = ATT_HEADS * HEAD_DIM
    kv_spec = pl.BlockSpec((None, None, width, seq), lambda b, g, i: (layer, b, g, 0))
    return pl.pallas_call(
        functools.partial(_attn_prompt_kernel, bias_base=layer * (aw // HEAD_DIM)),
        grid=(n_seq, aw // width, nq),
        in_specs=[
            pl.BlockSpec(memory_space=pltpu.SMEM),
            pl.BlockSpec((blk, width), lambda b, g, i: (b * nq + i, g)),
            kv_spec, kv_spec,
        ],
        out_specs=pl.BlockSpec((blk, width), lambda b, g, i: (b * nq + i, g)),
        out_shape=jax.ShapeDtypeStruct((n_seq * seq, aw), BF16),
        scratch_shapes=[pltpu.VMEM((ATT_HEADS // 2, blk, 2 * HEAD_DIM), F32),
                        pltpu.VMEM((ATT_HEADS, blk, 1), F32)],
        compiler_params=_cparams(("parallel", "parallel", "arbitrary")),
        name="attn_prompt",
    )(bias, q, kt, vt)


def _attn_sample_kernel(pt_ref, bias_ref, q_ref, kn_ref, vn_ref, *refs, n_pages, n_heads, page, tq,
                        bias_base):
    del pt_ref
    n_page_refs = ATT_SEQS * n_pages
    k_refs = refs[:n_page_refs]
    v_refs = refs[n_page_refs:2 * n_page_refs]
    o_ref = refs[2 * n_page_refs]
    hd = HEAD_DIM
    aw = n_heads * hd
    rows = n_heads * tq
    nt = (((1,), (1,)), ((), ()))
    seqs = range(ATT_SEQS)

    row_head = lax.broadcasted_iota(jnp.int32, (rows, aw), 0) // tq
    col_head = lax.broadcasted_iota(jnp.int32, (rows, aw), 1) // hd
    on_diag = row_head == col_head
    q_bd = [jnp.where(on_diag, jnp.concatenate([q_ref[s * tq:(s + 1) * tq, :]] * n_heads, axis=0),
                      0.0).astype(BF16) for s in seqs]
    bias_head = lax.broadcasted_iota(jnp.int32, (rows, 1), 0) // tq
    bias = jnp.zeros((rows, 1), F32)
    for h in range(n_heads):
        bias = jnp.where(bias_head == h, bias_ref[bias_base + h] * LOG2_E, bias)

    pad = jnp.zeros((page - tq, aw), F32)

    def new(ref, s):
        return jnp.concatenate([ref[s * tq:(s + 1) * tq, :], pad], axis=0).astype(BF16)

    def past(ref):
        return ref[...].reshape(aw, page).astype(BF16)

    tri = _later_key_matrix(page)

    n_blocks = n_pages + 1
    z = jnp.concatenate(
        [blk for s in seqs for blk in
         [lax.dot_general(q_bd[s], new(kn_ref, s), nt, preferred_element_type=F32) + bias]
         + [jnp.dot(q_bd[s], past(k_refs[s * n_pages + j]), preferred_element_type=F32) + bias
            for j in reversed(range(n_pages))]], axis=0)
    r = lax.broadcasted_iota(jnp.int32, (ATT_SEQS * n_blocks * rows, page), 0)
    c = lax.broadcasted_iota(jnp.int32, (ATT_SEQS * n_blocks * rows, page), 1)
    mask = ((r % (n_blocks * rows)) >= rows) | (c < (r % tq))
    log_beta, neg_log_keep = _sb_log2_terms(z)
    neg_log_keep = jnp.where(mask, neg_log_keep, 0.0)
    later = _split_dot(neg_log_keep, tri)
    tot = jnp.sum(neg_log_keep, axis=-1, keepdims=True)
    runs = []
    for s in seqs:
        run = jnp.zeros((rows, 1), F32)
        for b in range(n_blocks):
            runs.append(run)
            r0 = (s * n_blocks + b) * rows
            run = run + tot[r0:r0 + rows]
    w = jnp.where(mask, jnp.exp2(log_beta - later - jnp.concatenate(runs, axis=0)), 0.0)
    w = w.astype(BF16)

    outs = []
    for s in seqs:
        r0 = s * n_blocks * rows
        acc = jnp.dot(w[r0:r0 + rows], new(vn_ref, s), preferred_element_type=F32)
        for b in range(1, n_blocks):
            acc = acc + lax.dot_general(w[r0 + b * rows:r0 + (b + 1) * rows],
                                        past(v_refs[s * n_pages + n_pages - b]), nt,
                                        preferred_element_type=F32)
        outs.append(jnp.concatenate(
            [acc[h * tq:(h + 1) * tq, h * hd:(h + 1) * hd] for h in range(n_heads)], axis=-1))
    o_ref[...] = jnp.concatenate(outs, axis=0).astype(o_ref.dtype)


def _attn_sample(page_table, bias, q, k, v, cache_k, cache_v, layer, tq):
    n_dec, n_pages = page_table.shape
    _, _, n_heads, hd, page = cache_k.shape
    aw = n_heads * hd

    def page_spec(s, j):
        return pl.BlockSpec((None, None, n_heads, hd, page),
                            lambda b, pt: (layer, pt[(b * ATT_SEQS + s) * n_pages + j], 0, 0, 0))

    pages = [page_spec(s, j) for s in range(ATT_SEQS) for j in range(n_pages)]
    row = pl.BlockSpec((ATT_SEQS * tq, aw), lambda b, pt: (b, 0))
    grid_spec = pltpu.PrefetchScalarGridSpec(
        num_scalar_prefetch=1,
        grid=(n_dec // ATT_SEQS,),
        in_specs=[pl.BlockSpec(memory_space=pltpu.SMEM), row, row, row] + pages * 2,
        out_specs=row,
    )
    return pl.pallas_call(
        functools.partial(_attn_sample_kernel, n_pages=n_pages, n_heads=n_heads, page=page, tq=tq,
                          bias_base=layer * n_heads),
        grid_spec=grid_spec,
        out_shape=jax.ShapeDtypeStruct((n_dec * tq, aw), BF16),
        compiler_params=_cparams(("parallel",)),
        name="attn_sample",
    )(page_table.reshape(-1), bias, q, k, v, *([cache_k] * len(pages)), *([cache_v] * len(pages)))


def _discretize_kernel(are_ref, aim_ref, ldt_ref, bre_ref, bim_ref,
                       abre_ref, abim_ref, bbre_ref, bbim_ref):
    a_re = are_ref[...]
    a_im = aim_ref[...]
    dt = jnp.exp(ldt_ref[...])
    mag = jnp.exp(a_re * dt)
    ang = a_im * dt
    ab_re = mag * jnp.cos(ang)
    ab_im = mag * jnp.sin(ang)
    nr = ab_re - 1.0
    den = a_re * a_re + a_im * a_im
    c_re = (nr * a_re + ab_im * a_im) / den
    c_im = (ab_im * a_re - nr * a_im) / den
    b_re = bre_ref[...]
    b_im = bim_ref[...]
    abre_ref[...] = ab_re
    abim_ref[...] = ab_im
    bbre_ref[...] = c_re * b_re - c_im * b_im
    bbim_ref[...] = c_re * b_im + c_im * b_re


def _discretize(a_re, a_im, log_dt, b_re, b_im):
    depth, g, _, n = a_re.shape
    c = b_re.shape[2]
    sa = pl.BlockSpec((None, g, 1, n), lambda l: (l, 0, 0, 0))
    sb = pl.BlockSpec((None, g, c, n), lambda l: (l, 0, 0, 0))
    return pl.pallas_call(
        _discretize_kernel,
        grid=(depth,),
        in_specs=[sa, sa, pl.BlockSpec((None, g, 1, 1), lambda l: (l, 0, 0, 0)), sb, sb],
        out_specs=(sa, sa, sb, sb),
        out_shape=(jax.ShapeDtypeStruct((depth, g, 1, n), F32),) * 2
        + (jax.ShapeDtypeStruct((depth, g, c, n), F32),) * 2,
        compiler_params=_cparams(("parallel",)),
        name="ssm_discretize",
    )(a_re, a_im, log_dt, b_re, b_im)


def _ssm_kernel(u_ref, h0re_ref, h0im_ref, are_ref, aim_ref, bbd_ref, cre_ref, cim_ref,
                d_ref, wglu_ref, y_ref, hre_ref, him_ref, bu_ref, cre_s, cim_s, *, n_tiles, steps):
    tc = pl.program_id(1)
    st = SSM_SEQ_TILE

    @pl.when(tc == 0)
    def _():
        cre_s[...] = h0re_ref[...]
        cim_s[...] = h0im_ref[...]

    seqs = n_tiles * st
    u = jnp.swapaxes(u_ref[...], 0, 1).reshape(steps * seqs, u_ref.shape[2])
    ub = u.astype(BF16)
    n_blocks, cin, _ = bbd_ref.shape
    cs = SSM_COL_TILE
    for cg in range(n_blocks):
        bu_ref[:, 2 * cs * cg:2 * cs * (cg + 1)] = jnp.dot(
            ub[:, cin * cg:cin * (cg + 1)], bbd_ref[cg], preferred_element_type=F32)

    for cg in range(n_blocks):
        nat = slice(cg * cs, (cg + 1) * cs)
        re = slice(2 * cs * cg, 2 * cs * cg + cs)
        im = slice(2 * cs * cg + cs, 2 * cs * (cg + 1))
        ar = jnp.broadcast_to(are_ref[:, nat], (st, cs))
        ai = jnp.broadcast_to(aim_ref[:, nat], (st, cs))

        def tile(bt, _):
            s0 = pl.multiple_of(bt * st, st)

            def step(t, carry):
                hr, hi = carry
                r0 = pl.multiple_of((t * n_tiles + bt) * st, st)
                nhr = ar * hr - ai * hi + bu_ref[pl.ds(r0, st), re]
                nhi = ar * hi + ai * hr + bu_ref[pl.ds(r0, st), im]
                bu_ref[pl.ds(r0, st), re] = nhr
                bu_ref[pl.ds(r0, st), im] = nhi
                return nhr, nhi

            hr, hi = lax.fori_loop(0, steps, step,
                                   (cre_s[pl.ds(s0, st), nat], cim_s[pl.ds(s0, st), nat]),
                                   unroll=8)
            cre_s[pl.ds(s0, st), nat] = hr
            cim_s[pl.ds(s0, st), nat] = hi
            return 0

        lax.fori_loop(0, n_tiles, tile, 0)

    y = jnp.concatenate(
        [_bdot(bu_ref[:, 2 * cs * cg:2 * cs * cg + cs], cre_ref[cg])
         - _bdot(bu_ref[:, 2 * cs * cg + cs:2 * cs * (cg + 1)], cim_ref[cg])
         for cg in range(n_blocks)], axis=-1) + d_ref[...] * u
    yg = jax.nn.gelu(y)
    out = yg * jax.nn.sigmoid(_bdot(yg, wglu_ref[...]))
    y_ref[...] = jnp.swapaxes(out.reshape(steps, seqs, out.shape[1]), 0, 1).astype(y_ref.dtype)

    @pl.when(tc == pl.num_programs(1) - 1)
    def _():
        hre_ref[...] = cre_s[...]
        him_ref[...] = cim_s[...]


def _ssm(u, h0_re, h0_im, ab_re, ab_im, bbd, cre_bd, cim_bd, d_skip, w_glu, layer, n_tiles, steps):
    n_seq, seq, sw = u.shape
    ns = h0_re.shape[1]
    seqs = n_tiles * SSM_SEQ_TILE
    rows = steps * seqs
    block = pl.BlockSpec((seqs, steps, sw), lambda s, t: (s, t, 0))
    state = pl.BlockSpec((seqs, ns), lambda s, t: (s, 0))
    return pl.pallas_call(
        functools.partial(_ssm_kernel, n_tiles=n_tiles, steps=steps),
        grid=(n_seq // seqs, seq // steps),
        in_specs=[
            block, state, state,
            _layer_spec((1, ns), layer), _layer_spec((1, ns), layer),
            _layer_spec(bbd.shape[1:], layer), _layer_spec(cre_bd.shape[1:], layer),
            _layer_spec(cim_bd.shape[1:], layer),
            _layer_spec((1, sw), layer), _layer_spec((sw, sw), layer),
        ],
        out_specs=(block, state, state),
        out_shape=(jax.ShapeDtypeStruct((n_seq, seq, sw), BF16),
                   jax.ShapeDtypeStruct((n_seq, ns), F32),
                   jax.ShapeDtypeStruct((n_seq, ns), F32)),
        scratch_shapes=[pltpu.VMEM((rows, 2 * ns), F32),
                        pltpu.VMEM((seqs, ns), F32), pltpu.VMEM((seqs, ns), F32)],
        compiler_params=_cparams(("parallel", "arbitrary")),
        name="ssm",
    )(u, h0_re, h0_im, ab_re, ab_im, bbd, cre_bd, cim_bd, d_skip, w_glu)


def _merge_kernel(op_ref, os_ref, yp_ref, ys_ref, xp_ref, xs_ref, g_ref, wg_ref, wa_ref, ws_ref,
                  wo_ref, outp_ref, outs_ref, *, prompt_tiles):
    is_prompt = pl.program_id(0) < prompt_tiles
    x = _read_pair(is_prompt, xp_ref, xs_ref)
    d = x.shape[1]
    h = _rmsnorm(x, g_ref[...]).astype(BF16)
    g_att = jnp.dot(h, wg_ref[:, :d], preferred_element_type=F32)
    g_ssm = jnp.dot(h, wg_ref[:, d:], preferred_element_type=F32)
    att = jnp.dot(_read_pair(is_prompt, op_ref, os_ref), wa_ref[...], preferred_element_type=F32)
    ssm = jnp.dot(_read_pair(is_prompt, yp_ref, ys_ref), ws_ref[...], preferred_element_type=F32)
    merged = jax.nn.sigmoid(g_att) * att + jax.nn.sigmoid(g_ssm) * ssm
    _write_pair(is_prompt, outp_ref, outs_ref, x + _bdot(merged, wo_ref[...]))


def _merge(o_p, o_s, y_p, y_s, x_p, x_s, g, w_in, w_attn_up, w_ssm_up, w_out, layer):
    n_p, d = x_p.shape
    n_s = x_s.shape[0]
    aw = o_p.shape[1]
    sw = y_p.shape[1]
    pt = n_p // ROW_TILE
    return pl.pallas_call(
        functools.partial(_merge_kernel, prompt_tiles=pt),
        grid=(pt + n_s // ROW_TILE,),
        in_specs=_pair_specs(aw, pt) + _pair_specs(sw, pt) + _pair_specs(d, pt)
        + [_layer_spec((1, d), layer), _layer_spec((d, 2 * d), layer, col=w_in.shape[2] // (2 * d) - 1),
           _layer_spec((aw, d), layer), _layer_spec((sw, d), layer), _layer_spec((d, d), layer)],
        out_specs=_pair_specs(d, pt),
        out_shape=_pair_shapes(n_p, n_s, d, F32),
        compiler_params=_cparams(("arbitrary",)),
        name="merge",
    )(o_p, o_s, y_p, y_s, x_p, x_s, g, w_in, w_attn_up, w_ssm_up, w_out)


def _ffn_chunks(d_ff):
    chunk = 768
    edges = list(range(0, d_ff, chunk)) + [d_ff]
    return list(zip(edges[:-1], edges[1:]))


def _ffn_kernel(xp_ref, xs_ref, g_ref, wg_ref, wu_ref, wd_ref, *rest, prompt_tiles, final):
    outp_ref, outs_ref = rest[-2:]
    is_prompt = pl.program_id(0) < prompt_tiles
    x = _read_pair(is_prompt, xp_ref, xs_ref)
    h = _rmsnorm(x, g_ref[...]).astype(BF16)
    acc = x
    for c0, c1 in _ffn_chunks(wg_ref.shape[1]):
        gate = jnp.dot(h, wg_ref[:, c0:c1], preferred_element_type=F32)
        up = jnp.dot(h, wu_ref[:, c0:c1], preferred_element_type=F32)
        acc = acc + _bdot(jax.nn.silu(gate) * up, wd_ref[c0:c1, :])
    if final:
        acc = _rmsnorm(acc, rest[0][...])
    _write_pair(is_prompt, outp_ref, outs_ref, acc)


def _ffn(x_p, x_s, g, w_gate, w_up, w_down, layer, final_g=None):
    n_p, d = x_p.shape
    n_s = x_s.shape[0]
    d_ff = w_gate.shape[2]
    pt = n_p // ROW_TILE
    final = final_g is not None
    in_specs = _pair_specs(d, pt) + [_layer_spec((1, d), layer), _layer_spec((d, d_ff), layer),
                                     _layer_spec((d, d_ff), layer), _layer_spec((d_ff, d), layer)]
    args = [x_p, x_s, g, w_gate, w_up, w_down]
    if final:
        in_specs.append(_const_spec((1, d)))
        args.append(final_g)
    return pl.pallas_call(
        functools.partial(_ffn_kernel, prompt_tiles=pt, final=final),
        grid=(pt + n_s // ROW_TILE,),
        in_specs=in_specs,
        out_specs=_pair_specs(d, pt),
        out_shape=_pair_shapes(n_p, n_s, d, F32),
        compiler_params=_cparams(("arbitrary",)),
        name="ffn",
    )(*args)


def _block_diag(m):
    g, r, c = m.shape
    eye = jnp.eye(g, dtype=m.dtype)
    return (eye[:, None, :, None] * m[:, :, None, :]).reshape(g * r, g * c)


def kernel(x_prompt, x_sample, cache_k, cache_v, state_ssm_re, state_ssm_im, page_table, norm1_g, w_in, sb_bias, ssm_a_re, ssm_a_im, ssm_log_dt, ssm_b_re, ssm_b_im, ssm_c_re, ssm_c_im, ssm_d, ssm_w_glu, w_attn_up, w_ssm_up, w_out, norm2_g, w_ffn_gate, w_ffn_up, w_ffn_down, final_norm_g):
    n_seq_p, seq_p, d = x_prompt.shape
    n_seq_s, seq_s, _ = x_sample.shape
    depth = w_in.shape[0]
    _, groups, n_state = ssm_a_re.shape
    aw = w_attn_up.shape[1]
    sw = w_ssm_up.shape[1]
    n_heads = aw // HEAD_DIM
    ns = groups * n_state
    n_p = n_seq_p * seq_p
    n_s = n_seq_s * seq_s

    x_p = x_prompt.reshape(n_p, d)
    x_s = x_sample.reshape(n_s, d)

    ab_re, ab_im, bb_re, bb_im = _discretize(
        ssm_a_re[:, :, None, :], ssm_a_im[:, :, None, :], ssm_log_dt[:, :, None, None],
        ssm_b_re.transpose(0, 1, 3, 2), ssm_b_im.transpose(0, 1, 3, 2))

    cache_kt = cache_k.transpose(0, 1, 3, 4, 2)
    cache_vt = cache_v.transpose(0, 1, 3, 4, 2)

    gpb = SSM_COL_TILE // n_state
    n_cb = groups // gpb
    blocks = jax.vmap(lambda m: jax.vmap(_block_diag)(m.reshape(n_cb, gpb, *m.shape[1:])))

    assert 3 * aw + sw == 2 * d
    w_in_b = w_in.astype(BF16)
    w_kv_t = jnp.swapaxes(w_in_b[:, :, aw:3 * aw], 1, 2)
    g1 = norm1_g.reshape(depth, 1, d)
    g2 = norm2_g.reshape(depth, 1, d)
    w_attn_up_b, w_ssm_up_b, w_out_b = (w.astype(BF16) for w in (w_attn_up, w_ssm_up, w_out))
    w_gate_b, w_up_b, w_down_b = (w.astype(BF16) for w in (w_ffn_gate, w_ffn_up, w_ffn_down))
    ssm_w = (ab_re.reshape(depth, 1, ns), ab_im.reshape(depth, 1, ns),
             jnp.concatenate([blocks(bb_re), blocks(bb_im)], axis=3).astype(BF16),
             blocks(ssm_c_re.transpose(0, 1, 3, 2)).astype(BF16),
             blocks(ssm_c_im.transpose(0, 1, 3, 2)).astype(BF16),
             ssm_d.reshape(depth, 1, sw), ssm_w_glu.astype(BF16))
    h0_re = state_ssm_re.reshape(depth, n_seq_s, ns)
    h0_im = state_ssm_im.reshape(depth, n_seq_s, ns)
    bias = sb_bias.reshape(-1)

    zeros_p = jnp.zeros((n_seq_p, ns), F32)
    outs = [[] for _ in range(6)]
    kv_t = None
    for l in range(depth):
        q_p, q_s, u_p, u_s, k_s, v_s, kt, vt = _inproj(
            x_p, x_s, g1, w_in_b, w_kv_t, kv_t, l, n_seq_p, aw, sw)
        kv_t = (kt, vt)

        o_p = _attn_prompt(bias, q_p, kt, vt, l)
        o_s = _attn_sample(page_table, bias, q_s, k_s, v_s, cache_kt, cache_vt, l, seq_s)

        y_p, hre_p, him_p = _ssm(u_p.reshape(n_seq_p, seq_p, sw), zeros_p, zeros_p, *ssm_w, l,
                                 n_tiles=n_seq_p // SSM_SEQ_TILE, steps=SSM_TIME_TILE)
        y_s, hre_s, him_s = _ssm(u_s.reshape(n_seq_s, seq_s, sw), h0_re[l], h0_im[l], *ssm_w, l,
                                 n_tiles=n_seq_s // SSM_SEQ_TILE, steps=seq_s)
        x_p, x_s = _merge(o_p, o_s, y_p.reshape(n_p, sw), y_s.reshape(n_s, sw),
                          x_p, x_s, g1, w_in_b, w_attn_up_b, w_ssm_up_b, w_out_b, l)
        x_p, x_s = _ffn(x_p, x_s, g2, w_gate_b, w_up_b, w_down_b, l,
                        final_norm_g.reshape(1, d) if l == depth - 1 else None)

        for dst, val in zip(outs, (k_s, v_s, hre_p, him_p, hre_s, him_s)):
            dst.append(val)

    ks, vs, hrp, hip, hrs, his = (jnp.stack(o) for o in outs)
    heads_last = lambda a: a.reshape(depth, n_seq_p, n_heads, HEAD_DIM, seq_p).transpose(0, 1, 4, 2, 3)
    return (x_p.reshape(n_seq_p, seq_p, d),
            x_s.reshape(n_seq_s, seq_s, d),
            heads_last(kv_t[0]),
            heads_last(kv_t[1]),
            ks.reshape(depth, n_seq_s, seq_s, n_heads, HEAD_DIM),
            vs.reshape(depth, n_seq_s, seq_s, n_heads, HEAD_DIM),
            hrp.reshape(depth, n_seq_p, groups, n_state),
            hip.reshape(depth, n_seq_p, groups, n_state),
            hrs.reshape(depth, n_seq_s, groups, n_state),
            his.reshape(depth, n_seq_s, groups, n_state))
```

```python
import functools
import math

import jax
import jax.numpy as jnp
from jax import lax
from jax.experimental import pallas as pl
from jax.experimental.pallas import tpu as pltpu

F32 = jnp.float32
BF16 = jnp.bfloat16

HEAD_DIM = 64
RMS_EPS = 1e-6
LOG2_E = 1.0 / math.log(2.0)
Q_SCALE = HEAD_DIM ** -0.5 * LOG2_E
ROW_TILE = 512
ATT_BLOCK = 256
ATT_HEADS = 8
ATT_SEQS = 2
SSM_SEQ_TILE = 8
SSM_TIME_TILE = 64
SSM_COL_TILE = 512
VMEM_LIMIT = 56 * 1024 * 1024


def _cparams(sem):
    return pltpu.CompilerParams(dimension_semantics=sem, vmem_limit_bytes=VMEM_LIMIT)


def _const_spec(shape):
    nd = len(shape)
    return pl.BlockSpec(shape, lambda *_: (0,) * nd)


def _layer_spec(shape, layer, col=0):
    nd = len(shape)
    return pl.BlockSpec((None,) + tuple(shape), lambda *_: (layer,) + (0,) * (nd - 1) + (col,),
                        pipeline_mode=pl.Buffered(1))


def _rmsnorm(x, g):
    ms = jnp.mean(x * x, axis=-1, keepdims=True)
    return x * lax.rsqrt(ms + RMS_EPS) * g


def _bdot(a, b):
    return jnp.dot(a.astype(BF16), b.astype(BF16), preferred_element_type=F32)


def _sb_log2_terms(t):
    log_beta = jnp.minimum(t, 0.0) - jnp.log2(1.0 + jnp.exp2(-jnp.abs(t)))
    return log_beta, t - log_beta


def _split_dot(x, m):
    hi = x.astype(BF16)
    lo = (x - hi.astype(F32)).astype(BF16)
    return (jnp.dot(hi, m, preferred_element_type=F32)
            + jnp.dot(lo, m, preferred_element_type=F32))


def _pair_specs(width, prompt_tiles):
    prompt = pl.BlockSpec((ROW_TILE, width), lambda i: (jnp.minimum(i, prompt_tiles - 1), 0))
    sample = pl.BlockSpec((ROW_TILE, width), lambda i: (jnp.maximum(i - prompt_tiles, 0), 0))
    return [prompt, sample]


def _pair_shapes(n_p, n_s, width, dtype):
    return [jax.ShapeDtypeStruct((n_p, width), dtype), jax.ShapeDtypeStruct((n_s, width), dtype)]


def _read_pair(is_prompt, p_ref, s_ref):
    return jnp.where(is_prompt, p_ref[...], s_ref[...])


def _write_pair(is_prompt, p_ref, s_ref, value):
    @pl.when(is_prompt)
    def _():
        p_ref[...] = value

    @pl.when(jnp.logical_not(is_prompt))
    def _():
        s_ref[...] = value


def _inproj_kernel(xp_ref, xs_ref, g_ref, w_ref, wt_ref, *refs, aw, sw, prompt_tiles, aliased):
    (qp_ref, qs_ref, up_ref, us_ref, ks_ref, vs_ref, kt_ref, vt_ref) = refs[2 if aliased else 0:]
    is_prompt = pl.program_id(0) < prompt_tiles
    h = _rmsnorm(_read_pair(is_prompt, xp_ref, xs_ref), g_ref[...]).astype(BF16)

    def proj(c0, n):
        return jnp.dot(h, w_ref[:, c0:c0 + n], preferred_element_type=F32)

    _write_pair(is_prompt, qp_ref, qs_ref, proj(0, aw) * Q_SCALE)
    _write_pair(is_prompt, up_ref, us_ref, proj(3 * aw, sw))

    @pl.when(is_prompt)
    def _():
        nt = (((1,), (1,)), ((), ()))
        kv = lax.dot_general(wt_ref[...], h, nt, preferred_element_type=F32)
        kt_ref[...] = kv[:aw]
        vt_ref[...] = kv[aw:]

    @pl.when(jnp.logical_not(is_prompt))
    def _():
        ks_ref[...] = proj(aw, aw)
        vs_ref[...] = proj(2 * aw, aw)


def _inproj(x_p, x_s, g, w_in, w_kv_t, kv_t, layer, n_seq_p, aw, sw):
    depth = w_in.shape[0]
    n_p, d = x_p.shape
    n_s = x_s.shape[0]
    seq_p = n_p // n_seq_p
    seq_tiles = seq_p // ROW_TILE
    pt = n_p // ROW_TILE
    aliased = kv_t is not None

    def kt_map(i):
        j = jnp.minimum(i, pt - 1)
        return (layer, j // seq_tiles, 0, j % seq_tiles)

    kt_spec = pl.BlockSpec((None, None, aw, ROW_TILE), kt_map)
    stacked = jax.ShapeDtypeStruct((depth, n_seq_p, aw, seq_p), F32)
    sample_kv = _pair_specs(aw, pt)[1]
    any_spec = pl.BlockSpec(memory_space=pl.ANY)
    return pl.pallas_call(
        functools.partial(_inproj_kernel, aw=aw, sw=sw, prompt_tiles=pt, aliased=aliased),
        grid=(pt + n_s // ROW_TILE,),
        in_specs=_pair_specs(d, pt)
        + [_layer_spec((1, d), layer), _layer_spec((d, 3 * aw + sw), layer),
           _layer_spec((2 * aw, d), layer)]
        + ([any_spec, any_spec] if aliased else []),
        out_specs=_pair_specs(aw, pt) + _pair_specs(sw, pt) + [sample_kv, sample_kv, kt_spec, kt_spec],
        out_shape=_pair_shapes(n_p, n_s, aw, F32)
        + _pair_shapes(n_p, n_s, sw, F32)
        + _pair_shapes(n_p, n_s, aw, F32)[1:] * 2
        + [stacked, stacked],
        input_output_aliases={5: 6, 6: 7} if aliased else {},
        compiler_params=_cparams(("arbitrary",)),
        name="inproj",
    )(x_p, x_s, g, w_in, w_kv_t, *(kv_t if aliased else ()))


def _later_key_matrix(n):
    r = lax.broadcasted_iota(jnp.int32, (n, n), 0)
    c = lax.broadcasted_iota(jnp.int32, (n, n), 1)
    return (r > c).astype(BF16)


def _attn_prompt_kernel(bias_ref, q_ref, kt_ref, vt_ref, o_ref, acc_ref, run_ref, *, bias_base):
    grp = pl.program_id(1)
    qi = pl.program_id(2)
    blk = ATT_BLOCK
    pair = 2 * HEAD_DIM
    n_pairs = ATT_HEADS // 2
    lane = lax.broadcasted_iota(jnp.int32, (blk, pair), 1)
    in_head = (lane < HEAD_DIM, lane >= HEAD_DIM)
    feat = lax.broadcasted_iota(jnp.int32, (pair, blk), 0)
    v_rows = (feat < HEAD_DIM, feat >= HEAD_DIM)
    tri = _later_key_matrix(blk)
    r = lax.broadcasted_iota(jnp.int32, (blk, blk), 0)
    c = lax.broadcasted_iota(jnp.int32, (blk, blk), 1)
    causal = c < r
    nt = (((1,), (1,)), ((), ()))

    q = q_ref[...].astype(BF16)
    qh = [jnp.where(in_head[h % 2], q[:, (h // 2) * pair:(h // 2 + 1) * pair], 0).astype(BF16)
          for h in range(ATT_HEADS)]
    bias = [bias_ref[bias_base + grp * ATT_HEADS + h] * LOG2_E for h in range(ATT_HEADS)]
    acc_ref[...] = jnp.zeros_like(acc_ref)
    run_ref[...] = jnp.zeros_like(run_ref)

    def block(j, mask):
        start = pl.multiple_of(j * blk, blk)
        heads = range(ATT_HEADS)
        kb = [kt_ref[p * pair:(p + 1) * pair, pl.ds(start, blk)].astype(BF16)
              for p in range(n_pairs)]
        vb = [vt_ref[p * pair:(p + 1) * pair, pl.ds(start, blk)].astype(BF16)
              for p in range(n_pairs)]
        z = [jnp.dot(qh[h], kb[h // 2], preferred_element_type=F32) for h in heads]
        terms = [_sb_log2_terms(z[h] + bias[h]) for h in heads]
        log_beta = [t[0] for t in terms]
        keep = [t[1] if mask is None else jnp.where(mask, t[1], 0.0) for t in terms]
        later = [jnp.dot(keep[h].astype(BF16), tri, preferred_element_type=F32) for h in heads]
        w = [jnp.exp2(log_beta[h] - (later[h] + run_ref[h])) for h in heads]
        if mask is not None:
            w = [jnp.where(mask, w[h], 0.0) for h in heads]
        for h in heads:
            run_ref[h] = run_ref[h] + jnp.sum(keep[h], axis=-1, keepdims=True)
        for p in range(n_pairs):
            pv = [lax.dot_general(w[2 * p + hh].astype(BF16),
                                  jnp.where(v_rows[hh], vb[p], 0).astype(BF16), nt,
                                  preferred_element_type=F32) for hh in range(2)]
            acc_ref[p] = acc_ref[p] + (pv[0] + pv[1])

    block(qi, causal)

    def body(i, carry):
        block(qi - 1 - 2 * i, None)
        block(qi - 2 - 2 * i, None)
        return carry

    lax.fori_loop(0, qi // 2, body, 0)

    @pl.when(qi % 2 == 1)
    def _():
        block(0, None)
    o_ref[...] = jnp.concatenate([acc_ref[p] for p in range(n_pairs)], axis=-1).astype(o_ref.dtype)


def _attn_prompt(bias, q, kt, vt, layer):
    _, n_seq, aw, seq = kt.shape
    blk = ATT_BLOCK
    nq = seq // blk
    width = ATT_HEADS * HEAD_DIM
    kv_spec = pl.BlockSpec((None, None, width, seq), lambda b, g, i: (layer, b, g, 0))
    return pl.pallas_call(
        functools.partial(_attn_prompt_kernel, bias_base=layer * (aw // HEAD_DIM)),
        grid=(n_seq, aw // width, nq),
        in_specs=[
            pl.BlockSpec(memory_space=pltpu.SMEM),
            pl.BlockSpec((blk, width), lambda b, g, i: (b * nq + i, g)),
            kv_spec, kv_spec,
        ],
        out_specs=pl.BlockSpec((blk, width), lambda b, g, i: (b * nq + i, g)),
        out_shape=jax.ShapeDtypeStruct((n_seq * seq, aw), BF16),
        scratch_shapes=[pltpu.VMEM((ATT_HEADS // 2, blk, 2 * HEAD_DIM), F32),
                        pltpu.VMEM((ATT_HEADS, blk, 1), F32)],
        compiler_params=_cparams(("parallel", "parallel", "arbitrary")),
        name="attn_prompt",
    )(bias, q, kt, vt)


def _attn_sample_kernel(pt_ref, bias_ref, q_ref, kn_ref, vn_ref, *refs, n_pages, n_heads, page, tq,
                        bias_base):
    del pt_ref
    n_page_refs = ATT_SEQS * n_pages
    k_refs = refs[:n_page_refs]
    v_refs = refs[n_page_refs:2 * n_page_refs]
    o_ref = refs[2 * n_page_refs]
    hd = HEAD_DIM
    aw = n_heads * hd
    rows = n_heads * tq
    nt = (((1,), (1,)), ((), ()))
    seqs = range(ATT_SEQS)

    row_head = lax.broadcasted_iota(jnp.int32, (rows, aw), 0) // tq
    col_head = lax.broadcasted_iota(jnp.int32, (rows, aw), 1) // hd
    on_diag = row_head == col_head
    q_bd = [jnp.where(on_diag, jnp.concatenate([q_ref[s * tq:(s + 1) * tq, :]] * n_heads, axis=0),
                      0.0).astype(BF16) for s in seqs]
    bias_head = lax.broadcasted_iota(jnp.int32, (rows, 1), 0) // tq
    bias = jnp.zeros((rows, 1), F32)
    for h in range(n_heads):
        bias = jnp.where(bias_head == h, bias_ref[bias_base + h] * LOG2_E, bias)

    pad = jnp.zeros((page - tq, aw), F32)

    def new(ref, s):
        return jnp.concatenate([ref[s * tq:(s + 1) * tq, :], pad], axis=0).astype(BF16)

    def past(ref):
        return ref[...].reshape(aw, page).astype(BF16)

    tri = _later_key_matrix(page)

    n_blocks = n_pages + 1
    z = jnp.concatenate(
        [blk for s in seqs for blk in
         [lax.dot_general(q_bd[s], new(kn_ref, s), nt, preferred_element_type=F32) + bias]
         + [jnp.dot(q_bd[s], past(k_refs[s * n_pages + j]), preferred_element_type=F32) + bias
            for j in reversed(range(n_pages))]], axis=0)
    r = lax.broadcasted_iota(jnp.int32, (ATT_SEQS * n_blocks * rows, page), 0)
    c = lax.broadcasted_iota(jnp.int32, (ATT_SEQS * n_blocks * rows, page), 1)
    mask = ((r % (n_blocks * rows)) >= rows) | (c < (r % tq))
    log_beta, neg_log_keep = _sb_log2_terms(z)
    neg_log_keep = jnp.where(mask, neg_log_keep, 0.0)
    later = _split_dot(neg_log_keep, tri)
    tot = jnp.sum(neg_log_keep, axis=-1, keepdims=True)
    runs = []
    for s in seqs:
        run = jnp.zeros((rows, 1), F32)
        for b in range(n_blocks):
            runs.append(run)
            r0 = (s * n_blocks + b) * rows
            run = run + tot[r0:r0 + rows]
    w = jnp.where(mask, jnp.exp2(log_beta - later - jnp.concatenate(runs, axis=0)), 0.0)
    w = w.astype(BF16)

    outs = []
    for s in seqs:
        r0 = s * n_blocks * rows
        acc = jnp.dot(w[r0:r0 + rows], new(vn_ref, s), preferred_element_type=F32)
        for b in range(1, n_blocks):
            acc = acc + lax.dot_general(w[r0 + b * rows:r0 + (b + 1) * rows],
                                        past(v_refs[s * n_pages + n_pages - b]), nt,
                                        preferred_element_type=F32)
        outs.append(jnp.concatenate(
            [acc[h * tq:(h + 1) * tq, h * hd:(h + 1) * hd] for h in range(n_heads)], axis=-1))
    o_ref[...] = jnp.concatenate(outs, axis=0).astype(o_ref.dtype)


def _attn_sample(page_table, bias, q, k, v, cache_k, cache_v, layer, tq):
    n_dec, n_pages = page_table.shape
    _, _, n_heads, hd, page = cache_k.shape
    aw = n_heads * hd

    def page_spec(s, j):
        return pl.BlockSpec((None, None, n_heads, hd, page),
                            lambda b, pt: (layer, pt[(b * ATT_SEQS + s) * n_pages + j], 0, 0, 0))

    pages = [page_spec(s, j) for s in range(ATT_SEQS) for j in range(n_pages)]
    row = pl.BlockSpec((ATT_SEQS * tq, aw), lambda b, pt: (b, 0))
    grid_spec = pltpu.PrefetchScalarGridSpec(
        num_scalar_prefetch=1,
        grid=(n_dec // ATT_SEQS,),
        in_specs=[pl.BlockSpec(memory_space=pltpu.SMEM), row, row, row] + pages * 2,
        out_specs=row,
    )
    return pl.pallas_call(
        functools.partial(_attn_sample_kernel, n_pages=n_pages, n_heads=n_heads, page=page, tq=tq,
                          bias_base=layer * n_heads),
        grid_spec=grid_spec,
        out_shape=jax.ShapeDtypeStruct((n_dec * tq, aw), BF16),
        compiler_params=_cparams(("parallel",)),
        name="attn_sample",
    )(page_table.reshape(-1), bias, q, k, v, *([cache_k] * len(pages)), *([cache_v] * len(pages)))


def _discretize_kernel(are_ref, aim_ref, ldt_ref, bre_ref, bim_ref,
                       abre_ref, abim_ref, bbre_ref, bbim_ref):
    a_re = are_ref[...]
    a_im = aim_ref[...]
    dt = jnp.exp(ldt_ref[...])
    mag = jnp.exp(a_re * dt)
    ang = a_im * dt
    ab_re = mag * jnp.cos(ang)
    ab_im = mag * jnp.sin(ang)
    nr = ab_re - 1.0
    den = a_re * a_re + a_im * a_im
    c_re = (nr * a_re + ab_im * a_im) / den
    c_im = (ab_im * a_re - nr * a_im) / den
    b_re = bre_ref[...]
    b_im = bim_ref[...]
    abre_ref[...] = ab_re
    abim_ref[...] = ab_im
    bbre_ref[...] = c_re * b_re - c_im * b_im
    bbim_ref[...] = c_re * b_im + c_im * b_re


def _discretize(a_re, a_im, log_dt, b_re, b_im):
    depth, g, _, n = a_re.shape
    c = b_re.shape[2]
    sa = pl.BlockSpec((None, g, 1, n), lambda l: (l, 0, 0, 0))
    sb = pl.BlockSpec((None, g, c, n), lambda l: (l, 0, 0, 0))
    return pl.pallas_call(
        _discretize_kernel,
        grid=(depth,),
        in_specs=[sa, sa, pl.BlockSpec((None, g, 1, 1), lambda l: (l, 0, 0, 0)), sb, sb],
        out_specs=(sa, sa, sb, sb),
        out_shape=(jax.ShapeDtypeStruct((depth, g, 1, n), F32),) * 2
        + (jax.ShapeDtypeStruct((depth, g, c, n), F32),) * 2,
        compiler_params=_cparams(("parallel",)),
        name="ssm_discretize",
    )(a_re, a_im, log_dt, b_re, b_im)


def _ssm_kernel(u_ref, h0re_ref, h0im_ref, are_ref, aim_ref, bbd_ref, cre_ref, cim_ref,
                d_ref, wglu_ref, y_ref, hre_ref, him_ref, bu_ref, cre_s, cim_s, *, n_tiles, steps):
    tc = pl.program_id(1)
    st = SSM_SEQ_TILE

    @pl.when(tc == 0)
    def _():
        cre_s[...] = h0re_ref[...]
        cim_s[...] = h0im_ref[...]

    seqs = n_tiles * st
    u = jnp.swapaxes(u_ref[...], 0, 1).reshape(steps * seqs, u_ref.shape[2])
    ub = u.astype(BF16)
    n_blocks, cin, _ = bbd_ref.shape
    cs = SSM_COL_TILE
    for cg in range(n_blocks):
        bu_ref[:, 2 * cs * cg:2 * cs * (cg + 1)] = jnp.dot(
            ub[:, cin * cg:cin * (cg + 1)], bbd_ref[cg], preferred_element_type=F32)

    for cg in range(n_blocks):
        nat = slice(cg * cs, (cg + 1) * cs)
        re = slice(2 * cs * cg, 2 * cs * cg + cs)
        im = slice(2 * cs * cg + cs, 2 * cs * (cg + 1))
        ar = jnp.broadcast_to(are_ref[:, nat], (st, cs))
        ai = jnp.broadcast_to(aim_ref[:, nat], (st, cs))

        def tile(bt, _):
            s0 = pl.multiple_of(bt * st, st)

            def step(t, carry):
                hr, hi = carry
                r0 = pl.multiple_of((t * n_tiles + bt) * st, st)
                nhr = ar * hr - ai * hi + bu_ref[pl.ds(r0, st), re]
                nhi = ar * hi + ai * hr + bu_ref[pl.ds(r0, st), im]
                bu_ref[pl.ds(r0, st), re] = nhr
                bu_ref[pl.ds(r0, st), im] = nhi
                return nhr, nhi

            hr, hi = lax.fori_loop(0, steps, step,
                                   (cre_s[pl.ds(s0, st), nat], cim_s[pl.ds(s0, st), nat]),
                                   unroll=8)
            cre_s[pl.ds(s0, st), nat] = hr
            cim_s[pl.ds(s0, st), nat] = hi
            return 0

        lax.fori_loop(0, n_tiles, tile, 0)

    y = jnp.concatenate(
        [_bdot(bu_ref[:, 2 * cs * cg:2 * cs * cg + cs], cre_ref[cg])
         - _bdot(bu_ref[:, 2 * cs * cg + cs:2 * cs * (cg + 1)], cim_ref[cg])
         for cg in range(n_blocks)], axis=-1) + d_ref[...] * u
    yg = jax.nn.gelu(y)
    out = yg * jax.nn.sigmoid(_bdot(yg, wglu_ref[...]))
    y_ref[...] = jnp.swapaxes(out.reshape(steps, seqs, out.shape[1]), 0, 1).astype(y_ref.dtype)

    @pl.when(tc == pl.num_programs(1) - 1)
    def _():
        hre_ref[...] = cre_s[...]
        him_ref[...] = cim_s[...]


def _ssm(u, h0_re, h0_im, ab_re, ab_im, bbd, cre_bd, cim_bd, d_skip, w_glu, layer, n_tiles, steps):
    n_seq, seq, sw = u.shape
    ns = h0_re.shape[1]
    seqs = n_tiles * SSM_SEQ_TILE
    rows = steps * seqs
    block = pl.BlockSpec((seqs, steps, sw), lambda s, t: (s, t, 0))
    state = pl.BlockSpec((seqs, ns), lambda s, t: (s, 0))
    return pl.pallas_call(
        functools.partial(_ssm_kernel, n_tiles=n_tiles, steps=steps),
        grid=(n_seq // seqs, seq // steps),
        in_specs=[
            block, state, state,
            _layer_spec((1, ns), layer), _layer_spec((1, ns), layer),
            _layer_spec(bbd.shape[1:], layer), _layer_spec(cre_bd.shape[1:], layer),
            _layer_spec(cim_bd.shape[1:], layer),
            _layer_spec((1, sw), layer), _layer_spec((sw, sw), layer),
        ],
        out_specs=(block, state, state),
        out_shape=(jax.ShapeDtypeStruct((n_seq, seq, sw), BF16),
                   jax.ShapeDtypeStruct((n_seq, ns), F32),
                   jax.ShapeDtypeStruct((n_seq, ns), F32)),
        scratch_shapes=[pltpu.VMEM((rows, 2 * ns), F32),
                        pltpu.VMEM((seqs, ns), F32), pltpu.VMEM((seqs, ns), F32)],
        compiler_params=_cparams(("parallel", "arbitrary")),
        name="ssm",
    )(u, h0_re, h0_im, ab_re, ab_im, bbd, cre_bd, cim_bd, d_skip, w_glu)


def _merge_kernel(op_ref, os_ref, yp_ref, ys_ref, xp_ref, xs_ref, g_ref, wg_ref, wa_ref, ws_ref,
                  wo_ref, outp_ref, outs_ref, *, prompt_tiles):
    is_prompt = pl.program_id(0) < prompt_tiles
    x = _read_pair(is_prompt, xp_ref, xs_ref)
    d = x.shape[1]
    h = _rmsnorm(x, g_ref[...]).astype(BF16)
    g_att = jnp.dot(h, wg_ref[:, :d], preferred_element_type=F32)
    g_ssm = jnp.dot(h, wg_ref[:, d:], preferred_element_type=F32)
    att = jnp.dot(_read_pair(is_prompt, op_ref, os_ref), wa_ref[...], preferred_element_type=F32)
    ssm = jnp.dot(_read_pair(is_prompt, yp_ref, ys_ref), ws_ref[...], preferred_element_type=F32)
    merged = jax.nn.sigmoid(g_att) * att + jax.nn.sigmoid(g_ssm) * ssm
    _write_pair(is_prompt, outp_ref, outs_ref, x + _bdot(merged, wo_ref[...]))


def _merge(o_p, o_s, y_p, y_s, x_p, x_s, g, w_in, w_attn_up, w_ssm_up, w_out, layer):
    n_p, d = x_p.shape
    n_s = x_s.shape[0]
    aw = o_p.shape[1]
    sw = y_p.shape[1]
    pt = n_p // ROW_TILE
    return pl.pallas_call(
        functools.partial(_merge_kernel, prompt_tiles=pt),
        grid=(pt + n_s // ROW_TILE,),
        in_specs=_pair_specs(aw, pt) + _pair_specs(sw, pt) + _pair_specs(d, pt)
        + [_layer_spec((1, d), layer), _layer_spec((d, 2 * d), layer, col=w_in.shape[2] // (2 * d) - 1),
           _layer_spec((aw, d), layer), _layer_spec((sw, d), layer), _layer_spec((d, d), layer)],
        out_specs=_pair_specs(d, pt),
        out_shape=_pair_shapes(n_p, n_s, d, F32),
        compiler_params=_cparams(("arbitrary",)),
        name="merge",
    )(o_p, o_s, y_p, y_s, x_p, x_s, g, w_in, w_attn_up, w_ssm_up, w_out)


def _ffn_chunks(d_ff):
    chunk = 768
    edges = list(range(0, d_ff, chunk)) + [d_ff]
    return list(zip(edges[:-1], edges[1:]))


def _ffn_kernel(xp_ref, xs_ref, g_ref, wg_ref, wu_ref, wd_ref, *rest, prompt_tiles, final):
    outp_ref, outs_ref = rest[-2:]
    is_prompt = pl.program_id(0) < prompt_tiles
    x = _read_pair(is_prompt, xp_ref, xs_ref)
    h = _rmsnorm(x, g_ref[...]).astype(BF16)
    acc = x
    for c0, c1 in _ffn_chunks(wg_ref.shape[1]):
        gate = jnp.dot(h, wg_ref[:, c0:c1], preferred_element_type=F32)
        up = jnp.dot(h, wu_ref[:, c0:c1], preferred_element_type=F32)
        acc = acc + _bdot(jax.nn.silu(gate) * up, wd_ref[c0:c1, :])
    if final:
        acc = _rmsnorm(acc, rest[0][...])
    _write_pair(is_prompt, outp_ref, outs_ref, acc)


def _ffn(x_p, x_s, g, w_gate, w_up, w_down, layer, final_g=None):
    n_p, d = x_p.shape
    n_s = x_s.shape[0]
    d_ff = w_gate.shape[2]
    pt = n_p // ROW_TILE
    final = final_g is not None
    in_specs = _pair_specs(d, pt) + [_layer_spec((1, d), layer), _layer_spec((d, d_ff), layer),
                                     _layer_spec((d, d_ff), layer), _layer_spec((d_ff, d), layer)]
    args = [x_p, x_s, g, w_gate, w_up, w_down]
    if final:
        in_specs.append(_const_spec((1, d)))
        args.append(final_g)
    return pl.pallas_call(
        functools.partial(_ffn_kernel, prompt_tiles=pt, final=final),
        grid=(pt + n_s // ROW_TILE,),
        in_specs=in_specs,
        out_specs=_pair_specs(d, pt),
        out_shape=_pair_shapes(n_p, n_s, d, F32),
        compiler_params=_cparams(("arbitrary",)),
        name="ffn",
    )(*args)


def _block_diag(m):
    g, r, c = m.shape
    eye = jnp.eye(g, dtype=m.dtype)
    return (eye[:, None, :, None] * m[:, :, None, :]).reshape(g * r, g * c)


def kernel(x_prompt, x_sample, cache_k, cache_v, state_ssm_re, state_ssm_im, page_table, norm1_g, w_in, sb_bias, ssm_a_re, ssm_a_im, ssm_log_dt, ssm_b_re, ssm_b_im, ssm_c_re, ssm_c_im, ssm_d, ssm_w_glu, w_attn_up, w_ssm_up, w_out, norm2_g, w_ffn_gate, w_ffn_up, w_ffn_down, final_norm_g):
    n_seq_p, seq_p, d = x_prompt.shape
    n_seq_s, seq_s, _ = x_sample.shape
    depth = w_in.shape[0]
    _, groups, n_state = ssm_a_re.shape
    aw = w_attn_up.shape[1]
    sw = w_ssm_up.shape[1]
    n_heads = aw // HEAD_DIM
    ns = groups * n_state
    n_p = n_seq_p * seq_p
    n_s = n_seq_s * seq_s

    x_p = x_prompt.reshape(n_p, d)
    x_s = x_sample.reshape(n_s, d)

    ab_re, ab_im, bb_re, bb_im = _discretize(
        ssm_a_re[:, :, None, :], ssm_a_im[:, :, None, :], ssm_log_dt[:, :, None, None],
        ssm_b_re.transpose(0, 1, 3, 2), ssm_b_im.transpose(0, 1, 3, 2))

    cache_kt = cache_k.transpose(0, 1, 3, 4, 2)
    cache_vt = cache_v.transpose(0, 1, 3, 4, 2)

    gpb = SSM_COL_TILE // n_state
    n_cb = groups // gpb
    blocks = jax.vmap(lambda m: jax.vmap(_block_diag)(m.reshape(n_cb, gpb, *m.shape[1:])))

    assert 3 * aw + sw == 2 * d
    w_in_b = w_in.astype(BF16)
    w_kv_t = jnp.swapaxes(w_in_b[:, :, aw:3 * aw], 1, 2)
    g1 = norm1_g.reshape(depth, 1, d)
    g2 = norm2_g.reshape(depth, 1, d)
    w_attn_up_b, w_ssm_up_b, w_out_b = (w.astype(BF16) for w in (w_attn_up, w_ssm_up, w_out))
    w_gate_b, w_up_b, w_down_b = (w.astype(BF16) for w in (w_ffn_gate, w_ffn_up, w_ffn_down))
    ssm_w = (ab_re.reshape(depth, 1, ns), ab_im.reshape(depth, 1, ns),
             jnp.concatenate([blocks(bb_re), blocks(bb_im)], axis=3).astype(BF16),
             blocks(ssm_c_re.transpose(0, 1, 3, 2)).astype(BF16),
             blocks(ssm_c_im.transpose(0, 1, 3, 2)).astype(BF16),
             ssm_d.reshape(depth, 1, sw), ssm_w_glu.astype(BF16))
    h0_re = state_ssm_re.reshape(depth, n_seq_s, ns)
    h0_im = state_ssm_im.reshape(depth, n_seq_s, ns)
    bias = sb_bias.reshape(-1)

    zeros_p = jnp.zeros((n_seq_p, ns), F32)
    outs = [[] for _ in range(6)]
    kv_t = None
    for l in range(depth):
        q_p, q_s, u_p, u_s, k_s, v_s, kt, vt = _inproj(
            x_p, x_s, g1, w_in_b, w_kv_t, kv_t, l, n_seq_p, aw, sw)
        kv_t = (kt, vt)

        o_p = _attn_prompt(bias, q_p, kt, vt, l)
        o_s = _attn_sample(page_table, bias, q_s, k_s, v_s, cache_kt, cache_vt, l, seq_s)

        y_p, hre_p, him_p = _ssm(u_p.reshape(n_seq_p, seq_p, sw), zeros_p, zeros_p, *ssm_w, l,
                                 n_tiles=n_seq_p // SSM_SEQ_TILE, steps=SSM_TIME_TILE)
        y_s, hre_s, him_s = _ssm(u_s.reshape(n_seq_s, seq_s, sw), h0_re[l], h0_im[l], *ssm_w, l,
                                 n_tiles=n_seq_s // SSM_SEQ_TILE, steps=seq_s)
        x_p, x_s = _merge(o_p, o_s, y_p.reshape(n_p, sw), y_s.reshape(n_s, sw),
                          x_p, x_s, g1, w_in_b, w_attn_up_b, w_ssm_up_b, w_out_b, l)
        x_p, x_s = _ffn(x_p, x_s, g2, w_gate_b, w_up_b, w_down_b, l,
                        final_norm_g.reshape(1, d) if l == depth - 1 else None)

        for dst, val in zip(outs, (k_s, v_s, hre_p, him_p, hre_s, him_s)):
            dst.append(val)

    ks, vs, hrp, hip, hrs, his = (jnp.stack(o) for o in outs)
    heads_last = lambda a: a.reshape(depth, n_seq_p, n_heads, HEAD_DIM, seq_p).transpose(0, 1, 4, 2, 3)
    return (x_p.reshape(n_seq_p, seq_p, d),
            x_s.reshape(n_seq_s, seq_s, d),
            heads_last(kv_t[0]),
            heads_last(kv_t[1]),
            ks.reshape(depth, n_seq_s, seq_s, n_heads, HEAD_DIM),
            vs.reshape(depth, n_seq_s, seq_s, n_heads, HEAD_DIM),
            hrp.reshape(depth, n_seq_p, groups, n_state),
            hip.reshape(depth, n_seq_p, groups, n_state),
            hrs.reshape(depth, n_seq_s, groups, n_state),
            his.reshape(depth, n_seq_s, groups, n_state))
```
